```python
import jax, jax.numpy as jnp
from jax import lax
import numpy as np

D_MODEL = 2048
BATCH = 4
SEQ = 8192
DEPTH = 1

CHUNK = 64
EPS = 1e-6

RET_HEADS = 8
RET_DK = 256
RET_DV = 256
RET_QK_WIDTH = RET_HEADS * RET_DK
RET_WIDTH = RET_HEADS * RET_DV
ROPE_THETA = 10000.0

SSD_EXPAND = 2
SSD_WIDTH = SSD_EXPAND * D_MODEL
SSD_HEADDIM = 64
SSD_HEADS = SSD_WIDTH // SSD_HEADDIM
SSD_GROUPS = 8
SSD_HPG = SSD_HEADS // SSD_GROUPS
SSD_STATE = 128
SSD_CONV = 4
SSD_CONV_DIM = SSD_WIDTH + 2 * SSD_GROUPS * SSD_STATE
DT_MIN = 0.001
DT_MAX = 0.1

SPLITS = (RET_QK_WIDTH, RET_QK_WIDTH, RET_WIDTH, RET_WIDTH,
          SSD_WIDTH, SSD_CONV_DIM, SSD_HEADS, D_MODEL, D_MODEL)
IN_PROJ_DIM = sum(SPLITS)

kernel_name = "retention_ssd_gated_hybrid"

f32 = jnp.float32


def rmsnorm(x, w):
    xf = x.astype(f32)
    y = xf * lax.rsqrt(jnp.mean(xf * xf, axis=-1, keepdims=True) + EPS)
    return (y * w.astype(f32)).astype(x.dtype)


def to_chunks(t):
    b, s = t.shape[0], t.shape[1]
    return jnp.moveaxis(t.reshape((b, s // CHUNK, CHUNK) + t.shape[2:]), 1, 0)


def from_chunks(t):
    t = jnp.moveaxis(t, 0, 1)
    return t.reshape((t.shape[0], t.shape[1] * t.shape[2]) + t.shape[3:])


def rope(t, positions):
    half = t.shape[-1] // 2
    inv_freq = ROPE_THETA ** (-jnp.arange(half, dtype=f32) / half)
    ang = positions.astype(f32)[..., None] * inv_freq
    cos, sin = jnp.cos(ang)[:, :, None, :], jnp.sin(ang)[:, :, None, :]
    t1, t2 = t[..., :half], t[..., half:]
    return jnp.concatenate([t1 * cos - t2 * sin, t2 * cos + t1 * sin], axis=-1)


def retention(q, k, v, positions):
    b, s, _ = q.shape
    q = rope(q.reshape(b, s, RET_HEADS, RET_DK).astype(f32), positions)
    k = rope(k.reshape(b, s, RET_HEADS, RET_DK).astype(f32), positions) * (RET_DK ** -0.5)
    v = v.reshape(b, s, RET_HEADS, RET_DV).astype(f32)

    log_gamma = jnp.log1p(-(2.0 ** (-5.0 - jnp.arange(RET_HEADS, dtype=f32))))
    idx = jnp.arange(CHUNK, dtype=f32)
    intra = jnp.exp(jnp.abs(idx[:, None] - idx[None, :]) * log_gamma[:, None, None])
    q_decay = jnp.exp((idx[:, None] + 1.0) * log_gamma[None, :])[None, :, :, None]
    k_decay = jnp.exp((CHUNK - 1.0 - idx[:, None]) * log_gamma[None, :])[None, :, :, None]
    chunk_decay = jnp.exp(CHUNK * log_gamma)[None, :, None, None]

    def step(state, inp):
        qc, kc, vc = inp
        scores = jnp.einsum('blhd,bshd->bhls', qc, kc) * intra
        y = jnp.einsum('bhls,bshv->blhv', scores, vc)
        y = y + jnp.einsum('blhd,bhdv->blhv', qc, state) * q_decay
        state = state * chunk_decay + jnp.einsum('bshd,bshv->bhdv', kc * k_decay, vc)
        return state, y

    state0 = jnp.zeros((b, RET_HEADS, RET_DK, RET_DV), f32)
    _, y = lax.scan(step, state0, (to_chunks(q), to_chunks(k), to_chunks(v)))
    y = from_chunks(y)
    mu = jnp.mean(y, axis=-1, keepdims=True)
    var = jnp.mean(jnp.square(y - mu), axis=-1, keepdims=True)
    y = (y - mu) * lax.rsqrt(var + EPS)
    return y.reshape(b, s, RET_WIDTH)


def ssd(xbc, dt_raw, conv_w, conv_b, dt_bias, a_log, d_skip):
    b, s, _ = xbc.shape
    xbc = lax.conv_general_dilated(
        xbc.astype(f32), conv_w.astype(f32)[:, None, :], window_strides=(1,),
        padding=[(SSD_CONV - 1, 0)], dimension_numbers=('NWC', 'WIO', 'NWC'),
        feature_group_count=SSD_CONV_DIM) + conv_b.astype(f32)
    xbc = jax.nn.silu(xbc)
    gn = SSD_GROUPS * SSD_STATE
    xs = xbc[..., :SSD_WIDTH].reshape(b, s, SSD_GROUPS, SSD_HPG, SSD_HEADDIM)
    bm = xbc[..., SSD_WIDTH:SSD_WIDTH + gn].reshape(b, s, SSD_GROUPS, SSD_STATE)
    cm = xbc[..., SSD_WIDTH + gn:].reshape(b, s, SSD_GROUPS, SSD_STATE)
    dt = jax.nn.softplus(dt_raw.astype(f32) + dt_bias.astype(f32)).reshape(b, s, SSD_GROUPS, SSD_HPG)
    a = dt * (-jnp.exp(a_log.astype(f32))).reshape(SSD_GROUPS, SSD_HPG)
    xdt = xs * dt[..., None]
    causal = jnp.tril(jnp.ones((CHUNK, CHUNK), dtype=bool))[None, :, :, None, None]

    def step(state, inp):
        xc, bc, cc, ac = inp
        acum = jnp.cumsum(ac, axis=1)
        seg = acum[:, :, None] - acum[:, None, :]
        decay = jnp.exp(jnp.where(causal, seg, -jnp.inf))
        cb = jnp.einsum('blgn,bsgn->blsg', cc, bc)
        y = jnp.einsum('blsg,blsgh,bsghp->blghp', cb, decay, xc)
        y = y + jnp.einsum('blgn,bghpn->blghp', cc, state) * jnp.exp(acum)[..., None]
        tail = jnp.exp(acum[:, -1:] - acum)
        state = (state * jnp.exp(acum[:, -1])[..., None, None]
                 + jnp.einsum('bsgn,bsgh,bsghp->bghpn', bc, tail, xc))
        return state, y

    state0 = jnp.zeros((b, SSD_GROUPS, SSD_HPG, SSD_HEADDIM, SSD_STATE), f32)
    _, y = lax.scan(step, state0, (to_chunks(xdt), to_chunks(bm), to_chunks(cm), to_chunks(a)))
    y = from_chunks(y) + d_skip.astype(f32).reshape(SSD_GROUPS, SSD_HPG)[..., None] * xs
    return y.reshape(b, s, SSD_WIDTH)


def setup_inputs(seed: int = 0) -> dict:
    key = jax.random.key(seed)
    ks = jax.random.split(key, 16)
    x = jax.random.normal(ks[0], (BATCH, SEQ, D_MODEL), f32)
    offset = jax.random.randint(ks[1], (BATCH, 1), 0, 100000, dtype=jnp.int32)
    positions = offset + jnp.arange(SEQ, dtype=jnp.int32)[None, :]
    norm1_w = 1.0 + 0.02 * jax.random.normal(ks[2], (DEPTH, D_MODEL), f32)
    w_in = jax.random.normal(ks[3], (DEPTH, D_MODEL, IN_PROJ_DIM), f32) * D_MODEL ** -0.5
    conv_w = jax.random.normal(ks[4], (DEPTH, SSD_CONV, SSD_CONV_DIM), f32) * SSD_CONV ** -0.5
    conv_b = 0.01 * jax.random.normal(ks[5], (DEPTH, SSD_CONV_DIM), f32)
    u = jax.random.uniform(ks[6], (DEPTH, SSD_HEADS), f32)
    dt0 = jnp.exp(u * (np.log(DT_MAX) - np.log(DT_MIN)) + np.log(DT_MIN))
    dt_bias = dt0 + jnp.log(-jnp.expm1(-dt0))
    a_log = jnp.log(jax.random.uniform(ks[7], (DEPTH, SSD_HEADS), f32, minval=1.0, maxval=16.0))
    d_skip = 1.0 + 0.1 * jax.random.normal(ks[8], (DEPTH, SSD_HEADS), f32)
    ssd_norm_w = 1.0 + 0.02 * jax.random.normal(ks[9], (DEPTH, SSD_WIDTH), f32)
    w_br_ret = jax.random.normal(ks[10], (DEPTH, RET_WIDTH, D_MODEL), f32) * RET_WIDTH ** -0.5
    w_br_ssd = jax.random.normal(ks[11], (DEPTH, SSD_WIDTH, D_MODEL), f32) * SSD_WIDTH ** -0.5
    w_out = jax.random.normal(ks[12], (DEPTH, D_MODEL, D_MODEL), f32) * D_MODEL ** -0.5
    norm_f_w = 1.0 + 0.02 * jax.random.normal(ks[13], (D_MODEL,), f32)
    return {"x": x, "positions": positions, "norm1_w": norm1_w, "w_in": w_in,
            "conv_w": conv_w, "conv_b": conv_b, "dt_bias": dt_bias, "a_log": a_log,
            "d_skip": d_skip, "ssd_norm_w": ssd_norm_w, "w_br_ret": w_br_ret,
            "w_br_ssd": w_br_ssd, "w_out": w_out, "norm_f_w": norm_f_w}


def reference(x, positions, norm1_w, w_in, conv_w, conv_b, dt_bias, a_log, d_skip,
              ssd_norm_w, w_br_ret, w_br_ssd, w_out, norm_f_w):
    offsets = [int(o) for o in np.cumsum(SPLITS)[:-1]]
    for l in range(DEPTH):
        h = rmsnorm(x, norm1_w[l])
        proj = jnp.einsum('bsd,de->bse', h, w_in[l])
        q, k, v, g_ret, z, xbc, dt_raw, gate_r, gate_s = jnp.split(proj, offsets, axis=-1)
        y_r = (retention(q, k, v, positions) * jax.nn.silu(g_ret.astype(f32))).astype(x.dtype)
        y_s = ssd(xbc, dt_raw, conv_w[l], conv_b[l], dt_bias[l], a_log[l], d_skip[l])
        y_s = rmsnorm((y_s * jax.nn.silu(z.astype(f32))).astype(x.dtype), ssd_norm_w[l])
        p_r = jnp.einsum('bse,ed->bsd', y_r, w_br_ret[l])
        p_s = jnp.einsum('bse,ed->bsd', y_s, w_br_ssd[l])
        merged = jax.nn.sigmoid(gate_r) * p_r + jax.nn.sigmoid(gate_s) * p_s
        x = x + jnp.einsum('bsd,de->bse', merged, w_out[l])
    return rmsnorm(x, norm_f_w)
```

```python
import functools

import numpy as np
import jax
import jax.numpy as jnp
from jax import lax
from jax.experimental import pallas as pl
from jax.experimental.pallas import tpu as pltpu

f32 = jnp.float32
bf16 = jnp.bfloat16

D_MODEL = 2048
CHUNK = 64
EPS = 1e-6

RET_HEADS = 8
RET_DK = 256
RET_DV = 256
RET_WIDTH = RET_HEADS * RET_DV
ROPE_THETA = 10000.0

SSD_WIDTH = 2 * D_MODEL
SSD_HEADDIM = 64
SSD_HEADS = SSD_WIDTH // SSD_HEADDIM
SSD_GROUPS = 8
SSD_HPG = SSD_HEADS // SSD_GROUPS
SSD_STATE = 128
SSD_CONV = 4
SSD_GW = SSD_HPG * SSD_HEADDIM
SSD_CONV_DIM = SSD_WIDTH + 2 * SSD_GROUPS * SSD_STATE

COL_Q = 0
COL_K = COL_Q + RET_WIDTH
COL_V = COL_K + RET_WIDTH
COL_G = COL_V + RET_WIDTH
COL_Z = COL_G + RET_WIDTH
COL_X = COL_Z + SSD_WIDTH
COL_B = COL_X + SSD_WIDTH
COL_C = COL_B + SSD_GROUPS * SSD_STATE
COL_GR = COL_C + SSD_GROUPS * SSD_STATE
COL_GS = COL_GR + D_MODEL
N_MAIN = COL_GS + D_MODEL
DT_PAD = 128

LANES = 128
SUBLANES = 8
VMEM_LIMIT = 56 * 1024 * 1024

TM_IN = 1024
TN_IN = 1024
RET_BLK = 256
SSD_TB = 512
SSD_L = 128
TM_MERGE = 256
TM_OUT = 512


def _sigmoid(x):
    return 1.0 / (1.0 + jnp.exp(-x))


def _silu(x):
    return x * _sigmoid(x)


def _dot(a, b, ca=1, cb=0):
    return lax.dot_general(a, b, (((ca,), (cb,)), ((), ())), preferred_element_type=f32)


def _inproj_body(x_ref, nw_ref, w_ref, wdt_ref, o_ref, dt_ref, h_ref):
    @pl.when(pl.program_id(1) == 0)
    def _():
        x = x_ref[...]
        ms = jnp.mean(x * x, axis=-1, keepdims=True)
        h = (x * lax.rsqrt(ms + EPS) * nw_ref[...]).astype(bf16)
        h_ref[...] = h
        dt_ref[...] = _dot(h, wdt_ref[...]).T

    o_ref[...] = _dot(h_ref[...], w_ref[...]).astype(o_ref.dtype)


def _in_proj(x2, norm_w, w_main, w_dt):
    t = x2.shape[0]
    tm = min(TM_IN, t)
    assert t % tm == 0 and N_MAIN % TN_IN == 0
    return pl.pallas_call(
        _inproj_body,
        grid=(t // tm, N_MAIN // TN_IN),
        in_specs=[
            pl.BlockSpec((tm, D_MODEL), lambda i, j: (i, 0)),
            pl.BlockSpec((1, D_MODEL), lambda i, j: (0, 0)),
            pl.BlockSpec((D_MODEL, TN_IN), lambda i, j: (0, j)),
            pl.BlockSpec((D_MODEL, DT_PAD), lambda i, j: (0, 0)),
        ],
        out_specs=[
            pl.BlockSpec((tm, TN_IN), lambda i, j: (i, j)),
            pl.BlockSpec((DT_PAD, tm), lambda i, j: (0, i)),
        ],
        out_shape=[
            jax.ShapeDtypeStruct((t, N_MAIN), bf16),
            jax.ShapeDtypeStruct((DT_PAD, t), f32),
        ],
        scratch_shapes=[pltpu.VMEM((tm, D_MODEL), bf16)],
        compiler_params=pltpu.CompilerParams(
            dimension_semantics=("arbitrary", "arbitrary"),
            vmem_limit_bytes=VMEM_LIMIT),
        name="in_proj",
    )(x2, norm_w, w_main, w_dt)


def _retention_body(pos_ref, freq_ref, q_ref, k_ref, v_ref, g_ref,
                    dmask_ref, qdec_ref, kdec_ref, cdec_ref,
                    o_ref, state_ref, cos_ref, sin_ref):
    blk = pl.program_id(1)
    h = pl.program_id(2)
    half = RET_DK // 2

    @pl.when(h == 0)
    def _():
        ang = pos_ref[...] * freq_ref[...]
        cos_ref[...] = jnp.cos(ang)
        sin_ref[...] = jnp.sin(ang)

    @pl.when(blk == 0)
    def _():
        state_ref[h] = jnp.zeros((RET_DK, RET_DV), f32)

    cos = cos_ref[...]
    sin = sin_ref[...]

    def rope(t):
        t1 = t[:, :half]
        t2 = t[:, half:]
        return jnp.concatenate([t1 * cos - t2 * sin, t2 * cos + t1 * sin], axis=-1)

    q = rope(q_ref[...].astype(f32))
    k = rope(k_ref[...].astype(f32)) * (RET_DK ** -0.5)
    v = v_ref[...]
    qb = q.astype(bf16)
    kb = k.astype(bf16)
    kd = (k * kdec_ref[h]).astype(bf16)

    scores = _dot(qb, kb, 1, 1) * dmask_ref[h]
    state = state_ref[h]
    y = _dot(scores.astype(bf16), v)
    y = y + _dot(qb, state.astype(bf16)) * qdec_ref[h]
    state_ref[h] = state * cdec_ref[h] + _dot(kd, v, 0, 0)

    mu = jnp.mean(y, axis=-1, keepdims=True)
    yc = y - mu
    var = jnp.mean(yc * yc, axis=-1, keepdims=True)
    yn = yc * lax.rsqrt(var + EPS)
    o_ref[...] = (yn * _silu(g_ref[...].astype(f32))).astype(o_ref.dtype)


def _retention_tables():
    log_gamma = jnp.log1p(-(2.0 ** (-5.0 - jnp.arange(RET_HEADS, dtype=f32))))
    lg = log_gamma[:, None, None]
    n = jnp.arange(RET_BLK, dtype=f32)
    diff = n[:, None] - n[None, :]
    ci = (jnp.arange(RET_BLK) // CHUNK)
    same = ci[:, None] == ci[None, :]
    later = ci[:, None] > ci[None, :]
    expo = jnp.where(same[None], jnp.abs(diff)[None], diff[None]) * lg
    dmask = jnp.where((same | later)[None], jnp.exp(expo), 0.0).astype(f32)
    qdec = jnp.exp((n[None, :, None] + 1.0) * lg)
    kdec = jnp.exp((RET_BLK - 1.0 - n[None, :, None]) * lg)
    cdec = jnp.exp(RET_BLK * lg)
    qdec = jnp.broadcast_to(qdec, (RET_HEADS, RET_BLK, RET_DV))
    kdec = jnp.broadcast_to(kdec, (RET_HEADS, RET_BLK, RET_DK))
    cdec = jnp.broadcast_to(cdec, (RET_HEADS, RET_DK, RET_DV))
    return dmask, qdec, kdec, cdec


def _retention(proj, pos_b, batch, seq):
    t = batch * seq
    nblk = seq // RET_BLK
    assert seq % RET_BLK == 0
    half = RET_DK // 2
    inv_freq = (ROPE_THETA ** (-jnp.arange(half, dtype=f32) / half)).reshape(1, half)
    dmask, qdec, kdec, cdec = _retention_tables()
    cq, ck, cv, cg = (c // RET_DK for c in (COL_Q, COL_K, COL_V, COL_G))

    def rows(b, n, h):
        return b * nblk + n

    const3 = lambda b, n, h: (0, 0, 0)
    return pl.pallas_call(
        _retention_body,
        grid=(batch, nblk, RET_HEADS),
        in_specs=[
            pl.BlockSpec((RET_BLK, half), lambda b, n, h: (rows(b, n, h), 0)),
            pl.BlockSpec((1, half), lambda b, n, h: (0, 0)),
            pl.BlockSpec((RET_BLK, RET_DK), lambda b, n, h: (rows(b, n, h), cq + h)),
            pl.BlockSpec((RET_BLK, RET_DK), lambda b, n, h: (rows(b, n, h), ck + h)),
            pl.BlockSpec((RET_BLK, RET_DV), lambda b, n, h: (rows(b, n, h), cv + h)),
            pl.BlockSpec((RET_BLK, RET_DV), lambda b, n, h: (rows(b, n, h), cg + h)),
            pl.BlockSpec((RET_HEADS, RET_BLK, RET_BLK), const3),
            pl.BlockSpec((RET_HEADS, RET_BLK, RET_DV), const3),
            pl.BlockSpec((RET_HEADS, RET_BLK, RET_DK), const3),
            pl.BlockSpec((RET_HEADS, RET_DK, RET_DV), const3),
        ],
        out_specs=pl.BlockSpec((RET_BLK, RET_DV), lambda b, n, h: (rows(b, n, h), h)),
        out_shape=jax.ShapeDtypeStruct((t, RET_WIDTH), bf16),
        scratch_shapes=[
            pltpu.VMEM((RET_HEADS, RET_DK, RET_DV), f32),
            pltpu.VMEM((RET_BLK, half), f32),
            pltpu.VMEM((RET_BLK, half), f32),
        ],
        compiler_params=pltpu.CompilerParams(
            dimension_semantics=("arbitrary", "arbitrary", "arbitrary"),
            vmem_limit_bytes=VMEM_LIMIT),
        name="retention",
    )(pos_b, inv_freq, proj, proj, proj, proj, dmask, qdec, kdec, cdec)


def _cumsum_lanes(a):
    n = a.shape[-1]
    lane = lax.broadcasted_iota(jnp.int32, a.shape, 1)
    sh = 1
    while sh < n:
        a = a + jnp.where(lane >= sh, pltpu.roll(a, sh, axis=1), 0.0)
        sh *= 2
    return a


def _expand_heads(cols, lane_lo):
    rows = cols.shape[0]
    pieces = []
    for j in range(SSD_HPG // 2):
        a = jnp.broadcast_to(cols[:, 2 * j:2 * j + 1], (rows, LANES))
        b = jnp.broadcast_to(cols[:, 2 * j + 1:2 * j + 2], (rows, LANES))
        pieces.append(jnp.where(lane_lo, a, b))
    return jnp.concatenate(pieces, axis=-1)


def _ssd_body(xs_ref, b_ref, c_ref, z_ref, dt_ref,
              cwx_ref, cwb_ref, cwc_ref, cbx_ref, cbb_ref, cbc_ref,
              bias_ref, alog_ref, dexp_ref,
              o_ref, state_ref, xp_ref, bp_ref, cp_ref):
    blk = pl.program_id(2)
    tb = xs_ref.shape[0]
    L = SSD_L
    pad = SUBLANES

    @pl.when(blk == 0)
    def _():
        state_ref[...] = jnp.zeros(state_ref.shape, f32)
        xp_ref[0:pad, :] = jnp.zeros((pad, SSD_GW), f32)
        bp_ref[0:pad, :] = jnp.zeros((pad, SSD_STATE), f32)
        cp_ref[0:pad, :] = jnp.zeros((pad, SSD_STATE), f32)

    xp_ref[pad:pad + tb, :] = xs_ref[...].astype(f32)
    bp_ref[pad:pad + tb, :] = b_ref[...].astype(f32)
    cp_ref[pad:pad + tb, :] = c_ref[...].astype(f32)

    neg_a = -jnp.exp(alog_ref[...])
    bias = bias_ref[...]
    row = lax.broadcasted_iota(jnp.int32, (L, L), 0)
    col = lax.broadcasted_iota(jnp.int32, (L, L), 1)
    causal = row >= col
    lane_lo = lax.broadcasted_iota(jnp.int32, (L, LANES), 1) < SSD_HEADDIM
    lane_lo2 = jnp.concatenate([lane_lo, lane_lo], axis=0)

    def conv_silu(ref, w_ref, cb_ref, t0):
        acc = cb_ref[...]
        for j in range(SSD_CONV):
            acc = acc + ref[pad - (SSD_CONV - 1) + t0 + j:pad - (SSD_CONV - 1) + t0 + j + L, :] * w_ref[j:j + 1, :]
        return _silu(acc)

    for sb in range(tb // L):
        t0 = sb * L
        xc = conv_silu(xp_ref, cwx_ref, cbx_ref, t0)
        bc = conv_silu(bp_ref, cwb_ref, cbb_ref, t0)
        cc = conv_silu(cp_ref, cwc_ref, cbc_ref, t0)
        bcb = bc.astype(bf16)
        ccb = cc.astype(bf16)

        raw = dt_ref[:, t0:t0 + L] + bias
        dt = jnp.maximum(raw, 0.0) + jnp.log1p(jnp.exp(-jnp.abs(raw)))
        acum = _cumsum_lanes(dt * neg_a)
        alast = acum[:, L - 1:L]
        tailw = jnp.exp(alast - acum) * dt
        cols = jnp.concatenate([acum, tailw], axis=0).T
        acum_t = cols[:, :SSD_HPG]
        tailw_t = cols[:, SSD_HPG:]

        cb = _dot(ccb, bcb, 1, 1)

        xcb = xc.astype(bf16)
        ypieces = []
        for j in range(SSD_HPG // 2):
            gs = []
            for hh in (2 * j, 2 * j + 1):
                seg = jnp.broadcast_to(acum_t[:, hh:hh + 1], (L, L)) - acum[hh:hh + 1, :]
                dec = jnp.exp(jnp.where(causal, seg, -jnp.inf))
                gs.append((cb * dec * dt[hh:hh + 1, :]).astype(bf16))
            gpair = jnp.concatenate(gs, axis=1)
            xpair = xcb[:, j * LANES:(j + 1) * LANES]
            x2 = jnp.concatenate([xpair, xpair], axis=0)
            row_top = lax.broadcasted_iota(jnp.int32, (2 * L, LANES), 0) < L
            xbd = jnp.where(row_top == lane_lo2, x2, jnp.zeros_like(x2))
            ypieces.append(_dot(gpair, xbd))
        y = jnp.concatenate(ypieces, axis=-1)

        state = state_ref[...]
        ycross = _dot(ccb, state.astype(bf16))
        y = y + ycross * _expand_heads(jnp.exp(acum_t), lane_lo)
        y = y + dexp_ref[...] * xc

        xw = (xc * _expand_heads(tailw_t, lane_lo)).astype(bf16)
        sdec = _expand_heads(jnp.exp(acum_t[L - 1:L, :]), lane_lo[0:1, :])
        state_ref[...] = state * sdec + _dot(bcb, xw, 0, 0)

        zt = z_ref[t0:t0 + L, :].astype(f32)
        o_ref[t0:t0 + L, :] = (y * _silu(zt)).astype(o_ref.dtype)

    xp_ref[0:pad, :] = xp_ref[tb:tb + pad, :]
    bp_ref[0:pad, :] = bp_ref[tb:tb + pad, :]
    cp_ref[0:pad, :] = cp_ref[tb:tb + pad, :]


def _ssd(proj, dt_t, conv_w, conv_b, dt_bias, a_log, d_skip, batch, seq):
    t = batch * seq
    tb = min(SSD_TB, seq)
    nblk = seq // tb
    assert seq % tb == 0 and tb % SSD_L == 0
    L = SSD_L
    bias_b = jnp.broadcast_to(dt_bias.astype(f32)[:, None], (SSD_HEADS, L))
    alog_b = jnp.broadcast_to(a_log.astype(f32)[:, None], (SSD_HEADS, L))
    d_exp = jnp.repeat(d_skip.astype(f32), SSD_HEADDIM).reshape(1, SSD_WIDTH)
    conv_b2 = conv_b.astype(f32).reshape(1, SSD_CONV_DIM)
    conv_w = conv_w.astype(f32)

    cx = COL_X // SSD_GW
    cz = COL_Z // SSD_GW
    cbm = COL_B // SSD_STATE
    ccm = COL_C // SSD_STATE
    wb = SSD_WIDTH // SSD_STATE
    wc = wb + SSD_GROUPS

    def rows(b, g, n):
        return b * nblk + n

    return pl.pallas_call(
        _ssd_body,
        grid=(batch, SSD_GROUPS, nblk),
        in_specs=[
            pl.BlockSpec((tb, SSD_GW), lambda b, g, n: (rows(b, g, n), cx + g)),
            pl.BlockSpec((tb, SSD_STATE), lambda b, g, n: (rows(b, g, n), cbm + g)),
            pl.BlockSpec((tb, SSD_STATE), lambda b, g, n: (rows(b, g, n), ccm + g)),
            pl.BlockSpec((tb, SSD_GW), lambda b, g, n: (rows(b, g, n), cz + g)),
            pl.BlockSpec((SSD_HPG, tb), lambda b, g, n: (g, rows(b, g, n))),
            pl.BlockSpec((SSD_CONV, SSD_GW), lambda b, g, n: (0, g)),
            pl.BlockSpec((SSD_CONV, SSD_STATE), lambda b, g, n: (0, wb + g)),
            pl.BlockSpec((SSD_CONV, SSD_STATE), lambda b, g, n: (0, wc + g)),
            pl.BlockSpec((1, SSD_GW), lambda b, g, n: (0, g)),
            pl.BlockSpec((1, SSD_STATE), lambda b, g, n: (0, wb + g)),
            pl.BlockSpec((1, SSD_STATE), lambda b, g, n: (0, wc + g)),
            pl.BlockSpec((SSD_HPG, L), lambda b, g, n: (g, 0)),
            pl.BlockSpec((SSD_HPG, L), lambda b, g, n: (g, 0)),
            pl.BlockSpec((1, SSD_GW), lambda b, g, n: (0, g)),
        ],
        out_specs=pl.BlockSpec((tb, SSD_GW), lambda b, g, n: (rows(b, g, n), g)),
        out_shape=jax.ShapeDtypeStruct((t, SSD_WIDTH), bf16),
        scratch_shapes=[
            pltpu.VMEM((SSD_STATE, SSD_GW), f32),
            pltpu.VMEM((tb + SUBLANES, SSD_GW), f32),
            pltpu.VMEM((tb + SUBLANES, SSD_STATE), f32),
            pltpu.VMEM((tb + SUBLANES, SSD_STATE), f32),
        ],
        compiler_params=pltpu.CompilerParams(
            dimension_semantics=("arbitrary", "arbitrary", "arbitrary"),
            vmem_limit_bytes=VMEM_LIMIT),
        name="ssd",
    )(proj, proj, proj, proj, dt_t, conv_w, conv_w, conv_w, conv_b2, conv_b2, conv_b2,
      bias_b, alog_b, d_exp)


def _merge_body(yr_ref, ys_ref, gr_ref, gs_ref, nw_ref, wr_ref, ws_ref, o_ref):
    ys = ys_ref[...].astype(f32)
    ms = jnp.mean(ys * ys, axis=-1, keepdims=True)
    ysn = (ys * lax.rsqrt(ms + EPS) * nw_ref[...]).astype(bf16)
    p_r = _dot(yr_ref[...], wr_ref[...])
    p_s = _dot(ysn, ws_ref[...])
    merged = _sigmoid(gr_ref[...].astype(f32)) * p_r + _sigmoid(gs_ref[...].astype(f32)) * p_s
    o_ref[...] = merged.astype(o_ref.dtype)


def _merge(y_r, y_s, proj, ssd_norm_w, w_r, w_s):
    t = y_r.shape[0]
    tm = min(TM_MERGE, t)
    assert t % tm == 0
    cgr = COL_GR // D_MODEL
    cgs = COL_GS // D_MODEL
    const = lambda i: (0, 0)
    return pl.pallas_call(
        _merge_body,
        grid=(t // tm,),
        in_specs=[
            pl.BlockSpec((tm, RET_WIDTH), lambda i: (i, 0)),
            pl.BlockSpec((tm, SSD_WIDTH), lambda i: (i, 0)),
            pl.BlockSpec((tm, D_MODEL), lambda i: (i, cgr)),
            pl.BlockSpec((tm, D_MODEL), lambda i: (i, cgs)),
            pl.BlockSpec((1, SSD_WIDTH), const),
            pl.BlockSpec((RET_WIDTH, D_MODEL), const, pipeline_mode=pl.Buffered(1)),
            pl.BlockSpec((SSD_WIDTH, D_MODEL), const, pipeline_mode=pl.Buffered(1)),
        ],
        out_specs=pl.BlockSpec((tm, D_MODEL), lambda i: (i, 0)),
        out_shape=jax.ShapeDtypeStruct((t, D_MODEL), bf16),
        compiler_params=pltpu.CompilerParams(
            dimension_semantics=("arbitrary",),
            vmem_limit_bytes=VMEM_LIMIT),
        name="merge",
    )(y_r, y_s, proj, proj, ssd_norm_w, w_r, w_s)


def _out_body(x_ref, m_ref, wo_ref, nw_ref, o_ref):
    xo = x_ref[...] + _dot(m_ref[...], wo_ref[...])
    ms = jnp.mean(xo * xo, axis=-1, keepdims=True)
    o_ref[...] = xo * lax.rsqrt(ms + EPS) * nw_ref[...]


def _out_proj(x2, merged, w_o, norm_f_w):
    t = x2.shape[0]
    tm = min(TM_OUT, t)
    assert t % tm == 0
    const = lambda i: (0, 0)
    return pl.pallas_call(
        _out_body,
        grid=(t // tm,),
        in_specs=[
            pl.BlockSpec((tm, D_MODEL), lambda i: (i, 0)),
            pl.BlockSpec((tm, D_MODEL), lambda i: (i, 0)),
            pl.BlockSpec((D_MODEL, D_MODEL), const, pipeline_mode=pl.Buffered(1)),
            pl.BlockSpec((1, D_MODEL), const),
        ],
        out_specs=pl.BlockSpec((tm, D_MODEL), lambda i: (i, 0)),
        out_shape=jax.ShapeDtypeStruct((t, D_MODEL), f32),
        compiler_params=pltpu.CompilerParams(
            dimension_semantics=("arbitrary",),
            vmem_limit_bytes=VMEM_LIMIT),
        name="out_proj",
    )(x2, merged, w_o, norm_f_w)


def kernel(x, positions, norm1_w, w_in, conv_w, conv_b, dt_bias, a_log, d_skip,
           ssd_norm_w, w_br_ret, w_br_ssd, w_out, norm_f_w):
    batch, seq, d = x.shape
    assert d == D_MODEL and w_in.shape[0] == 1
    t = batch * seq
    x2 = x.reshape(t, d)

    w = w_in[0]
    dt0 = COL_GR
    w_main = jnp.concatenate([w[:, :dt0], w[:, dt0 + SSD_HEADS:]], axis=1).astype(bf16)
    w_dt = jnp.pad(w[:, dt0:dt0 + SSD_HEADS], ((0, 0), (0, DT_PAD - SSD_HEADS))).astype(bf16)
    pos_b = jnp.broadcast_to(positions.astype(f32).reshape(t, 1), (t, RET_DK // 2))

    proj, dt_t = _in_proj(x2, norm1_w[0].reshape(1, d), w_main, w_dt)
    y_r = _retention(proj, pos_b, batch, seq)
    y_s = _ssd(proj, dt_t, conv_w[0], conv_b[0], dt_bias[0], a_log[0], d_skip[0], batch, seq)
    merged = _merge(y_r, y_s, proj, ssd_norm_w[0].reshape(1, SSD_WIDTH),
                    w_br_ret[0].astype(bf16), w_br_ssd[0].astype(bf16))
    out = _out_proj(x2, merged, w_out[0].astype(bf16), norm_f_w.reshape(1, d))
    return out.reshape(batch, seq, d)
```

```python
import functools
import math

import jax
import jax.numpy as jnp
from jax import lax
from jax.experimental import pallas as pl
from jax.experimental.pallas import tpu as pltpu

f32 = jnp.float32
bf16 = jnp.bfloat16

D_MODEL = 2048
CHUNK = 64
EPS = 1e-6

RET_HEADS = 8
RET_DK = 256
RET_DV = 256
RET_WIDTH = RET_HEADS * RET_DV
ROPE_THETA = 10000.0
ROPE_HALF = RET_DK // 2

SSD_WIDTH = 2 * D_MODEL
SSD_HEADDIM = 64
SSD_HEADS = SSD_WIDTH // SSD_HEADDIM
SSD_GROUPS = 8
SSD_HPG = SSD_HEADS // SSD_GROUPS
SSD_STATE = 128
SSD_CONV = 4
SSD_GW = SSD_HPG * SSD_HEADDIM
SSD_CONV_DIM = SSD_WIDTH + 2 * SSD_GROUPS * SSD_STATE

OFF_Q = 0
OFF_K = OFF_Q + RET_WIDTH
OFF_V = OFF_K + RET_WIDTH
OFF_G = OFF_V + RET_WIDTH
OFF_Z = OFF_G + RET_WIDTH
OFF_XBC = OFF_Z + SSD_WIDTH
OFF_DT = OFF_XBC + SSD_CONV_DIM
OFF_GATES = OFF_DT + SSD_HEADS
DT_PAD = 128

LANES = 128
SUBLANES = 8
VMEM_LIMIT = 56 * 1024 * 1024
LOG2E = 1.0 / math.log(2.0)

TM_PRE = 512
TM_IN = 1024
TN_IN = 1024
SLAB = 256
RET_BLK = 256
SSD_TB = 512
SSD_L = 128
TM_MERGE = 256
TM_OUT = 512


def _sigmoid(x):
    return 1.0 / (1.0 + jnp.exp2(x * (-LOG2E)))


def _silu(x):
    return x * _sigmoid(x)


def _dot(a, b, ca=1, cb=0):
    return lax.dot_general(a, b, (((ca,), (cb,)), ((), ())), preferred_element_type=f32)


def _params(n_axes):
    return pltpu.CompilerParams(dimension_semantics=("arbitrary",) * n_axes,
                                vmem_limit_bytes=VMEM_LIMIT)


def _prenorm_body(x_ref, nw_ref, wdt_ref, pos_ref, freq_ref, h_ref, dt_ref, cos_ref, sin_ref):
    x = x_ref[...]
    ms = jnp.mean(x * x, axis=-1, keepdims=True)
    h = (x * lax.rsqrt(ms + EPS) * nw_ref[...]).astype(bf16)
    h_ref[...] = h
    dt_ref[...] = _dot(h, wdt_ref[...]).T
    ang = pos_ref[...] * freq_ref[...]
    cos_ref[...] = jnp.cos(ang)
    sin_ref[...] = jnp.sin(ang)


def _prenorm(x2, norm_w, w_dt, pos_b, inv_freq):
    t = x2.shape[0]
    tm = min(TM_PRE, t)
    assert t % tm == 0
    const = lambda i: (0, 0)
    return pl.pallas_call(
        _prenorm_body,
        grid=(t // tm,),
        in_specs=[
            pl.BlockSpec((tm, D_MODEL), lambda i: (i, 0)),
            pl.BlockSpec((1, D_MODEL), const),
            pl.BlockSpec((D_MODEL, DT_PAD), const),
            pl.BlockSpec((tm, ROPE_HALF), lambda i: (i, 0)),
            pl.BlockSpec((1, ROPE_HALF), const),
        ],
        out_specs=[
            pl.BlockSpec((tm, D_MODEL), lambda i: (i, 0)),
            pl.BlockSpec((DT_PAD, tm), lambda i: (0, i)),
            pl.BlockSpec((tm, ROPE_HALF), lambda i: (i, 0)),
            pl.BlockSpec((tm, ROPE_HALF), lambda i: (i, 0)),
        ],
        out_shape=[
            jax.ShapeDtypeStruct((t, D_MODEL), bf16),
            jax.ShapeDtypeStruct((DT_PAD, t), f32),
            jax.ShapeDtypeStruct((t, ROPE_HALF), f32),
            jax.ShapeDtypeStruct((t, ROPE_HALF), f32),
        ],
        compiler_params=_params(1),
        name="prenorm",
    )(x2, norm_w, w_dt, pos_b, inv_freq)


def _proj_body(*refs, epilogue, tiles_per_seq, k_first_col_block):
    if epilogue == "rope":
        h_ref, w_ref, cos_ref, sin_ref, o_ref, wb_ref = refs
    elif epilogue == "conv":
        h_ref, w_ref, cw_ref, cb_ref, o_ref, wb_ref, halo_ref = refs
    else:
        h_ref, w_ref, o_ref, wb_ref = refs
    j = pl.program_id(0)
    i = pl.program_id(1)
    tm, tn = o_ref.shape

    @pl.when(i == 0)
    def _():
        wb_ref[...] = w_ref[...].astype(bf16)

    if epilogue == "conv":
        @pl.when(i % tiles_per_seq == 0)
        def _():
            halo_ref[...] = jnp.zeros((SUBLANES, tn), f32)

    if epilogue == "rope":
        scale = jnp.where(j >= k_first_col_block, RET_DK ** -0.5, 1.0).astype(f32)
        cos = cos_ref[...] * scale
        sin = sin_ref[...] * scale

    for n in range(tn // SLAB):
        c0 = n * SLAB
        acc = _dot(h_ref[...], wb_ref[:, c0:c0 + SLAB])
        if epilogue == "none":
            out = acc
        elif epilogue == "silu":
            out = _silu(acc)
        elif epilogue == "sigmoid":
            out = _sigmoid(acc)
        elif epilogue == "rope":
            t1 = acc[:, :ROPE_HALF]
            t2 = acc[:, ROPE_HALF:]
            out = jnp.concatenate([t1 * cos - t2 * sin, t2 * cos + t1 * sin], axis=-1)
        elif epilogue == "conv":
            ext = jnp.concatenate([halo_ref[:, c0:c0 + SLAB], acc], axis=0)
            w = cw_ref[:, c0:c0 + SLAB]
            x1 = pltpu.roll(ext, 1, axis=0)
            a2 = pltpu.roll(ext * w[1:2] + x1 * w[0:1], 2, axis=0)
            out = _silu((ext * w[3:4] + x1 * w[2:3] + a2 + cb_ref[:, c0:c0 + SLAB])[SUBLANES:])
            halo_ref[:, c0:c0 + SLAB] = acc[tm - SUBLANES:tm]
        o_ref[:, c0:c0 + SLAB] = out.astype(o_ref.dtype)


def _proj(h, w, col0, ncols, epilogue, seq, extra=()):
    t = h.shape[0]
    tm = min(TM_IN, t, seq)
    tn = TN_IN
    assert t % tm == 0 and seq % tm == 0 and ncols % tn == 0 and col0 % tn == 0
    jb = col0 // tn
    in_specs = [
        pl.BlockSpec((tm, D_MODEL), lambda j, i: (i, 0)),
        pl.BlockSpec((D_MODEL, tn), lambda j, i: (0, jb + j)),
    ]
    scratch = [pltpu.VMEM((D_MODEL, tn), bf16)]
    if epilogue == "rope":
        in_specs += [pl.BlockSpec((tm, ROPE_HALF), lambda j, i: (i, 0))] * 2
    elif epilogue == "conv":
        cj = (col0 - OFF_XBC) // tn
        in_specs += [pl.BlockSpec((SSD_CONV, tn), lambda j, i: (0, cj + j)),
                     pl.BlockSpec((1, tn), lambda j, i: (0, cj + j))]
        scratch.append(pltpu.VMEM((SUBLANES, tn), f32))
    body = functools.partial(_proj_body, epilogue=epilogue, tiles_per_seq=seq // tm,
                             k_first_col_block=(OFF_K - col0) // tn)
    return pl.pallas_call(
        body,
        grid=(ncols // tn, t // tm),
        in_specs=in_specs,
        out_specs=pl.BlockSpec((tm, tn), lambda j, i: (i, j)),
        out_shape=jax.ShapeDtypeStruct((t, ncols), bf16),
        scratch_shapes=scratch,
        compiler_params=_params(2),
        name="proj_" + epilogue,
    )(h, w, *extra)


def _retention_body(q_ref, k_ref, v_ref, g_ref, dmask_ref, qdec_ref, kdec_ref, cdec_ref,
                    o_ref, state_ref):
    blk = pl.program_id(1)
    h = pl.program_id(2)

    @pl.when(blk == 0)
    def _():
        state_ref[h] = jnp.zeros((RET_DK, RET_DV), f32)

    qb = q_ref[...]
    kb = k_ref[...]
    v = v_ref[...]
    kd = (kb.astype(f32) * kdec_ref[h]).astype(bf16)

    scores = _dot(qb, kb, 1, 1) * dmask_ref[h]
    state = state_ref[h]
    y = _dot(scores.astype(bf16), v)
    y = y + _dot(qb, state.astype(bf16)) * qdec_ref[h]
    state_ref[h] = state * cdec_ref[h] + _dot(kd, v, 0, 0)

    mu = jnp.mean(y, axis=-1, keepdims=True)
    yc = y - mu
    var = jnp.mean(yc * yc, axis=-1, keepdims=True)
    yn = yc * lax.rsqrt(var + EPS)
    o_ref[...] = (yn * g_ref[...].astype(f32)).astype(o_ref.dtype)


def _retention_tables():
    log_gamma = jnp.log1p(-(2.0 ** (-5.0 - jnp.arange(RET_HEADS, dtype=f32))))
    lg = log_gamma[:, None, None]
    n = jnp.arange(RET_BLK, dtype=f32)
    diff = n[:, None] - n[None, :]
    ci = (jnp.arange(RET_BLK) // CHUNK)
    same = ci[:, None] == ci[None, :]
    later = ci[:, None] > ci[None, :]
    expo = jnp.where(same[None], jnp.abs(diff)[None], diff[None]) * lg
    dmask = jnp.where((same | later)[None], jnp.exp(expo), 0.0).astype(f32)
    qdec = jnp.exp((n[None, :, None] + 1.0) * lg)
    kdec = jnp.exp((RET_BLK - 1.0 - n[None, :, None]) * lg)
    cdec = jnp.exp(RET_BLK * lg)
    qdec = jnp.broadcast_to(qdec, (RET_HEADS, RET_BLK, RET_DV))
    kdec = jnp.broadcast_to(kdec, (RET_HEADS, RET_BLK, RET_DK))
    cdec = jnp.broadcast_to(cdec, (RET_HEADS, RET_DK, RET_DV))
    return dmask, qdec, kdec, cdec


def _retention(qk, v, gz, batch, seq):
    t = batch * seq
    nblk = seq // RET_BLK
    assert seq % RET_BLK == 0
    dmask, qdec, kdec, cdec = _retention_tables()

    def rows(b, n, h):
        return b * nblk + n

    const3 = lambda b, n, h: (0, 0, 0)
    return pl.pallas_call(
        _retention_body,
        grid=(batch, nblk, RET_HEADS),
        in_specs=[
            pl.BlockSpec((RET_BLK, RET_DK), lambda b, n, h: (rows(b, n, h), h)),
            pl.BlockSpec((RET_BLK, RET_DK), lambda b, n, h: (rows(b, n, h), RET_HEADS + h)),
            pl.BlockSpec((RET_BLK, RET_DV), lambda b, n, h: (rows(b, n, h), h)),
            pl.BlockSpec((RET_BLK, RET_DV), lambda b, n, h: (rows(b, n, h), h)),
            pl.BlockSpec((RET_HEADS, RET_BLK, RET_BLK), const3),
            pl.BlockSpec((RET_HEADS, RET_BLK, RET_DV), const3),
            pl.BlockSpec((RET_HEADS, RET_BLK, RET_DK), const3),
            pl.BlockSpec((RET_HEADS, RET_DK, RET_DV), const3),
        ],
        out_specs=pl.BlockSpec((RET_BLK, RET_DV), lambda b, n, h: (rows(b, n, h), h)),
        out_shape=jax.ShapeDtypeStruct((t, RET_WIDTH), bf16),
        scratch_shapes=[pltpu.VMEM((RET_HEADS, RET_DK, RET_DV), f32)],
        compiler_params=_params(3),
        name="retention",
    )(qk, qk, v, gz, dmask, qdec, kdec, cdec)


def _split3(a):
    hi = a.astype(bf16).astype(f32)
    r = a - hi
    mid = r.astype(bf16).astype(f32)
    lo = (r - mid).astype(bf16).astype(f32)
    return [hi, mid, lo]


def _expand_heads(cols, lane_lo):
    rows = cols.shape[0]
    pieces = []
    for j in range(SSD_HPG // 2):
        a = jnp.broadcast_to(cols[:, 2 * j:2 * j + 1], (rows, LANES))
        b = jnp.broadcast_to(cols[:, 2 * j + 1:2 * j + 2], (rows, LANES))
        pieces.append(jnp.where(lane_lo, a, b))
    return jnp.concatenate(pieces, axis=-1)


def _ssd_body(xs_ref, b_ref, c_ref, z_ref, dt_ref, bias_ref, alog_ref, dexp_ref,
              o_ref, state_ref):
    blk = pl.program_id(2)
    tb = xs_ref.shape[0]
    L = SSD_L
    nsb = tb // L
    H = SSD_HPG

    @pl.when(blk == 0)
    def _():
        state_ref[...] = jnp.zeros(state_ref.shape, f32)

    neg_a = -jnp.exp(alog_ref[...])
    bias = bias_ref[...]
    row = lax.broadcasted_iota(jnp.int32, (L, L), 0)
    col = lax.broadcasted_iota(jnp.int32, (L, L), 1)
    causal = row >= col
    upper = jnp.where(row <= col, 1.0, 0.0).astype(bf16)
    lane_lo = lax.broadcasted_iota(jnp.int32, (L, LANES), 1) < SSD_HEADDIM
    lane_lo2 = jnp.concatenate([lane_lo, lane_lo], axis=0)
    row_top = lax.broadcasted_iota(jnp.int32, (2 * L, LANES), 0) < L
    keep_bd = row_top == lane_lo2

    dts, parts = [], []
    for sb in range(nsb):
        raw = dt_ref[:, sb * L:(sb + 1) * L] + bias
        dt = jnp.maximum(raw, 0.0) + jnp.log1p(jnp.exp(-jnp.abs(raw)))
        dts.append(dt)
        parts += _split3(dt * neg_a)
    csum = _dot(jnp.concatenate(parts, axis=0).astype(bf16), upper)

    srows, colss = [], []
    for sb in range(nsb):
        dt = dts[sb]
        p0 = sb * 3 * H
        acum = (csum[p0:p0 + H] + csum[p0 + H:p0 + 2 * H]) + csum[p0 + 2 * H:p0 + 3 * H]
        alast = acum[:, L - 1:L]
        tailw = jnp.exp(alast - acum) * dt
        acum2 = acum * LOG2E
        srows.append(acum2 - jnp.maximum(jnp.log(dt) * LOG2E, -200.0))
        colss.append(jnp.concatenate([acum2, tailw, jnp.exp2(acum2)], axis=0).T)

    for sb in range(nsb):
        t0 = sb * L
        srow = srows[sb]
        acum2_t = colss[sb][:, :H]
        tailw_t = colss[sb][:, H:2 * H]
        eexp_t = colss[sb][:, 2 * H:]
        xcb = xs_ref[t0:t0 + L, :]
        bcb = b_ref[t0:t0 + L, :]
        ccb = c_ref[t0:t0 + L, :]
        xc = xcb.astype(f32)

        cb = _dot(ccb, bcb, 1, 1)

        ypieces = []
        for j in range(SSD_HPG // 2):
            gs = []
            for hh in (2 * j, 2 * j + 1):
                seg = jnp.broadcast_to(acum2_t[:, hh:hh + 1], (L, L)) - srow[hh:hh + 1, :]
                gs.append((cb * jnp.exp2(jnp.where(causal, seg, -jnp.inf))).astype(bf16))
            gpair = jnp.concatenate(gs, axis=1)
            xpair = xcb[:, j * LANES:(j + 1) * LANES]
            x2 = jnp.concatenate([xpair, xpair], axis=0)
            xbd = jnp.where(keep_bd, x2, jnp.zeros_like(x2))
            ypieces.append(_dot(gpair, xbd))
        y = jnp.concatenate(ypieces, axis=-1)

        state = state_ref[...]
        ycross = _dot(ccb, state.astype(bf16))
        y = y + ycross * _expand_heads(eexp_t, lane_lo)
        y = y + dexp_ref[...] * xc

        xw = (xc * _expand_heads(tailw_t, lane_lo)).astype(bf16)
        sdec = _expand_heads(eexp_t[L - 1:L, :], lane_lo[0:1, :])
        state_ref[...] = state * sdec + _dot(bcb, xw, 0, 0)

        o_ref[t0:t0 + L, :] = (y * z_ref[t0:t0 + L, :].astype(f32)).astype(o_ref.dtype)


def _ssd(xbc, gz, dt_t, dt_bias, a_log, d_skip, batch, seq):
    t = batch * seq
    tb = min(SSD_TB, seq)
    nblk = seq // tb
    assert seq % tb == 0 and tb % SSD_L == 0
    L = SSD_L
    bias_b = jnp.broadcast_to(dt_bias.astype(f32)[:, None], (SSD_HEADS, L))
    alog_b = jnp.broadcast_to(a_log.astype(f32)[:, None], (SSD_HEADS, L))
    d_exp = jnp.repeat(d_skip.astype(f32), SSD_HEADDIM).reshape(1, SSD_WIDTH)

    cz = RET_WIDTH // SSD_GW
    cbm = SSD_WIDTH // SSD_STATE
    ccm = cbm + SSD_GROUPS

    def rows(b, g, n):
        return b * nblk + n

    return pl.pallas_call(
        _ssd_body,
        grid=(batch, SSD_GROUPS, nblk),
        in_specs=[
            pl.BlockSpec((tb, SSD_GW), lambda b, g, n: (rows(b, g, n), g)),
            pl.BlockSpec((tb, SSD_STATE), lambda b, g, n: (rows(b, g, n), cbm + g)),
            pl.BlockSpec((tb, SSD_STATE), lambda b, g, n: (rows(b, g, n), ccm + g)),
            pl.BlockSpec((tb, SSD_GW), lambda b, g, n: (rows(b, g, n), cz + g)),
            pl.BlockSpec((SSD_HPG, tb), lambda b, g, n: (g, rows(b, g, n))),
            pl.BlockSpec((SSD_HPG, L), lambda b, g, n: (g, 0)),
            pl.BlockSpec((SSD_HPG, L), lambda b, g, n: (g, 0)),
            pl.BlockSpec((1, SSD_GW), lambda b, g, n: (0, g)),
        ],
        out_specs=pl.BlockSpec((tb, SSD_GW), lambda b, g, n: (rows(b, g, n), g)),
        out_shape=jax.ShapeDtypeStruct((t, SSD_WIDTH), bf16),
        scratch_shapes=[pltpu.VMEM((SSD_STATE, SSD_GW), f32)],
        compiler_params=_params(3),
        name="ssd",
    )(xbc, xbc, xbc, gz, dt_t, bias_b, alog_b, d_exp)


def _merge_body(yr_ref, ys_ref, gr_ref, gs_ref, nw_ref, wr_ref, ws_ref, o_ref):
    ys = ys_ref[...].astype(f32)
    ms = jnp.mean(ys * ys, axis=-1, keepdims=True)
    ysn = (ys * lax.rsqrt(ms + EPS) * nw_ref[...]).astype(bf16)
    p_r = _dot(yr_ref[...], wr_ref[...])
    p_s = _dot(ysn, ws_ref[...])
    merged = gr_ref[...].astype(f32) * p_r + gs_ref[...].astype(f32) * p_s
    o_ref[...] = merged.astype(o_ref.dtype)


def _merge(y_r, y_s, gates, ssd_norm_w, w_r, w_s):
    t = y_r.shape[0]
    tm = min(TM_MERGE, t)
    assert t % tm == 0
    const = lambda i: (0, 0)
    return pl.pallas_call(
        _merge_body,
        grid=(t // tm,),
        in_specs=[
            pl.BlockSpec((tm, RET_WIDTH), lambda i: (i, 0)),
            pl.BlockSpec((tm, SSD_WIDTH), lambda i: (i, 0)),
            pl.BlockSpec((tm, D_MODEL), lambda i: (i, 0)),
            pl.BlockSpec((tm, D_MODEL), lambda i: (i, 1)),
            pl.BlockSpec((1, SSD_WIDTH), const),
            pl.BlockSpec((RET_WIDTH, D_MODEL), const, pipeline_mode=pl.Buffered(1)),
            pl.BlockSpec((SSD_WIDTH, D_MODEL), const, pipeline_mode=pl.Buffered(1)),
        ],
        out_specs=pl.BlockSpec((tm, D_MODEL), lambda i: (i, 0)),
        out_shape=jax.ShapeDtypeStruct((t, D_MODEL), bf16),
        compiler_params=_params(1),
        name="merge",
    )(y_r, y_s, gates, gates, ssd_norm_w, w_r, w_s)


def _out_body(x_ref, m_ref, wo_ref, nw_ref, o_ref):
    xo = x_ref[...] + _dot(m_ref[...], wo_ref[...])
    ms = jnp.mean(xo * xo, axis=-1, keepdims=True)
    o_ref[...] = xo * lax.rsqrt(ms + EPS) * nw_ref[...]


def _out_proj(x2, merged, w_o, norm_f_w):
    t = x2.shape[0]
    tm = min(TM_OUT, t)
    assert t % tm == 0
    const = lambda i: (0, 0)
    return pl.pallas_call(
        _out_body,
        grid=(t // tm,),
        in_specs=[
            pl.BlockSpec((tm, D_MODEL), lambda i: (i, 0)),
            pl.BlockSpec((tm, D_MODEL), lambda i: (i, 0)),
            pl.BlockSpec((D_MODEL, D_MODEL), const, pipeline_mode=pl.Buffered(1)),
            pl.BlockSpec((1, D_MODEL), const),
        ],
        out_specs=pl.BlockSpec((tm, D_MODEL), lambda i: (i, 0)),
        out_shape=jax.ShapeDtypeStruct((t, D_MODEL), f32),
        compiler_params=_params(1),
        name="out_proj",
    )(x2, merged, w_o, norm_f_w)


def kernel(x, positions, norm1_w, w_in, conv_w, conv_b, dt_bias, a_log, d_skip,
           ssd_norm_w, w_br_ret, w_br_ssd, w_out, norm_f_w):
    batch, seq, d = x.shape
    assert d == D_MODEL and w_in.shape[0] == 1
    t = batch * seq
    x2 = x.reshape(t, d)
    w = w_in[0]

    w_dt = jnp.pad(w[:, OFF_DT:OFF_DT + SSD_HEADS], ((0, 0), (0, DT_PAD - SSD_HEADS))).astype(bf16)
    w_gates = w[:, OFF_GATES:]
    pos_b = jnp.broadcast_to(positions.astype(f32).reshape(t, 1), (t, ROPE_HALF))
    inv_freq = (ROPE_THETA ** (-jnp.arange(ROPE_HALF, dtype=f32) / ROPE_HALF)).reshape(1, ROPE_HALF)

    h, dt_t, cos, sin = _prenorm(x2, norm1_w[0].reshape(1, d), w_dt, pos_b, inv_freq)
    qk = _proj(h, w, OFF_Q, 2 * RET_WIDTH, "rope", seq, (cos, sin))
    v = _proj(h, w, OFF_V, RET_WIDTH, "none", seq)
    gz = _proj(h, w, OFF_G, RET_WIDTH + SSD_WIDTH, "silu", seq)
    xbc = _proj(h, w, OFF_XBC, SSD_CONV_DIM, "conv", seq,
                (conv_w[0].astype(f32), conv_b[0].astype(f32).reshape(1, SSD_CONV_DIM)))
    gates = _proj(h, w_gates, 0, 2 * D_MODEL, "sigmoid", seq)

    y_r = _retention(qk, v, gz, batch, seq)
    y_s = _ssd(xbc, gz, dt_t, dt_bias[0], a_log[0], d_skip[0], batch, seq)
    merged = _merge(y_r, y_s, gates, ssd_norm_w[0].reshape(1, SSD_WIDTH),
                    w_br_ret[0].astype(bf16), w_br_ssd[0].astype(bf16))
    out = _out_proj(x2, merged, w_out[0].astype(bf16), norm_f_w.reshape(1, d))
    return out.reshape(batch, seq, d)
```

```python
import functools
import math

import numpy as np
import jax
import jax.numpy as jnp
from jax import lax
from jax.experimental import pallas as pl
from jax.experimental.pallas import tpu as pltpu

f32 = jnp.float32
bf16 = jnp.bfloat16

D_MODEL = 2048
CHUNK = 64
EPS = 1e-6

RET_HEADS = 8
RET_DK = 256
RET_DV = 256
RET_WIDTH = RET_HEADS * RET_DV
ROPE_THETA = 10000.0
ROPE_HALF = RET_DK // 2

SSD_WIDTH = 2 * D_MODEL
SSD_HEADDIM = 64
SSD_HEADS = SSD_WIDTH // SSD_HEADDIM
SSD_GROUPS = 8
SSD_HPG = SSD_HEADS // SSD_GROUPS
SSD_STATE = 128
SSD_CONV = 4
SSD_GW = SSD_HPG * SSD_HEADDIM
SSD_CONV_DIM = SSD_WIDTH + 2 * SSD_GROUPS * SSD_STATE

OFF_Q = 0
OFF_K = OFF_Q + RET_WIDTH
OFF_V = OFF_K + RET_WIDTH
OFF_G = OFF_V + RET_WIDTH
OFF_Z = OFF_G + RET_WIDTH
OFF_XBC = OFF_Z + SSD_WIDTH
OFF_DT = OFF_XBC + SSD_CONV_DIM
OFF_GATES = OFF_DT + SSD_HEADS
DT_PAD = 128

LANES = 128
SUBLANES = 8
VMEM_LIMIT = 56 * 1024 * 1024
LOG2E = 1.0 / math.log(2.0)

TM_PRE = 512
TM_IN = 1024
TN_IN = 1024
SLAB = 256
RET_BLK = 256
RET_HB = 4
SSD_TB = 512
SSD_L = 128
TM_MERGE = 256
TM_OUT = 512


def _sigmoid(x):
    return 1.0 / (1.0 + jnp.exp2(x * (-LOG2E)))


def _silu(x):
    return x * _sigmoid(x)


def _dot(a, b, ca=1, cb=0):
    return lax.dot_general(a, b, (((ca,), (cb,)), ((), ())), preferred_element_type=f32)


def _params(n_axes):
    return pltpu.CompilerParams(dimension_semantics=("arbitrary",) * n_axes,
                                vmem_limit_bytes=VMEM_LIMIT)


def _prenorm_body(x_ref, nw_ref, wdt_ref, pos_ref, freq_ref, h_ref, dt_ref, cos_ref, sin_ref):
    x = x_ref[...]
    ms = jnp.mean(x * x, axis=-1, keepdims=True)
    h = (x * lax.rsqrt(ms + EPS) * nw_ref[...]).astype(bf16)
    h_ref[...] = h
    dt_ref[...] = _dot(h, wdt_ref[...]).T
    ang = pos_ref[...] * freq_ref[...]
    cos_ref[...] = jnp.cos(ang)
    sin_ref[...] = jnp.sin(ang)


def _prenorm(x2, norm_w, w_dt, pos_b, inv_freq):
    t = x2.shape[0]
    tm = min(TM_PRE, t)
    assert t % tm == 0
    const = lambda i: (0, 0)
    return pl.pallas_call(
        _prenorm_body,
        grid=(t // tm,),
        in_specs=[
            pl.BlockSpec((tm, D_MODEL), lambda i: (i, 0)),
            pl.BlockSpec((1, D_MODEL), const),
            pl.BlockSpec((D_MODEL, DT_PAD), const),
            pl.BlockSpec((tm, ROPE_HALF), lambda i: (i, 0)),
            pl.BlockSpec((1, ROPE_HALF), const),
        ],
        out_specs=[
            pl.BlockSpec((tm, D_MODEL), lambda i: (i, 0)),
            pl.BlockSpec((DT_PAD, tm), lambda i: (0, i)),
            pl.BlockSpec((tm, ROPE_HALF), lambda i: (i, 0)),
            pl.BlockSpec((tm, ROPE_HALF), lambda i: (i, 0)),
        ],
        out_shape=[
            jax.ShapeDtypeStruct((t, D_MODEL), bf16),
            jax.ShapeDtypeStruct((DT_PAD, t), f32),
            jax.ShapeDtypeStruct((t, ROPE_HALF), f32),
            jax.ShapeDtypeStruct((t, ROPE_HALF), f32),
        ],
        compiler_params=_params(1),
        name="prenorm",
    )(x2, norm_w, w_dt, pos_b, inv_freq)


def _proj_body(*refs, epilogue, tiles_per_seq, k_first_col_block):
    if epilogue == "rope":
        h_ref, w_ref, cos_ref, sin_ref, o_ref, wb_ref = refs
    elif epilogue == "conv":
        h_ref, w_ref, cw_ref, cb_ref, o_ref, wb_ref, halo_ref = refs
    else:
        h_ref, w_ref, o_ref, wb_ref = refs
    j = pl.program_id(0)
    i = pl.program_id(1)
    tm, tn = o_ref.shape

    @pl.when(i == 0)
    def _():
        wb_ref[...] = w_ref[...].astype(bf16)

    if epilogue == "conv":
        @pl.when(i % tiles_per_seq == 0)
        def _():
            halo_ref[...] = jnp.zeros((SUBLANES, tn), f32)

    if epilogue == "rope":
        scale = jnp.where(j >= k_first_col_block, RET_DK ** -0.5, 1.0).astype(f32)
        cos = cos_ref[...] * scale
        sin = sin_ref[...] * scale

    for n in range(tn // SLAB):
        c0 = n * SLAB
        acc = _dot(h_ref[...], wb_ref[:, c0:c0 + SLAB])
        if epilogue == "none":
            out = acc
        elif epilogue == "silu":
            out = _silu(acc)
        elif epilogue == "sigmoid":
            out = _sigmoid(acc)
        elif epilogue == "rope":
            t1 = acc[:, :ROPE_HALF]
            t2 = acc[:, ROPE_HALF:]
            out = jnp.concatenate([t1 * cos - t2 * sin, t2 * cos + t1 * sin], axis=-1)
        elif epilogue == "conv":
            ext = jnp.concatenate([halo_ref[:, c0:c0 + SLAB], acc], axis=0)
            w = cw_ref[:, c0:c0 + SLAB]
            x1 = pltpu.roll(ext, 1, axis=0)
            a2 = pltpu.roll(ext * w[1:2] + x1 * w[0:1], 2, axis=0)
            out = _silu((ext * w[3:4] + x1 * w[2:3] + a2 + cb_ref[:, c0:c0 + SLAB])[SUBLANES:])
            halo_ref[:, c0:c0 + SLAB] = acc[tm - SUBLANES:tm]
        o_ref[:, c0:c0 + SLAB] = out.astype(o_ref.dtype)


def _proj(h, w, col0, ncols, epilogue, seq, extra=()):
    t = h.shape[0]
    tm = min(TM_IN, t, seq)
    tn = TN_IN
    assert t % tm == 0 and seq % tm == 0 and ncols % tn == 0 and col0 % tn == 0
    jb = col0 // tn
    in_specs = [
        pl.BlockSpec((tm, D_MODEL), lambda j, i: (i, 0)),
        pl.BlockSpec((None, D_MODEL, tn), lambda j, i: (0, 0, jb + j)),
    ]
    scratch = [pltpu.VMEM((D_MODEL, tn), bf16)]
    if epilogue == "rope":
        in_specs += [pl.BlockSpec((tm, ROPE_HALF), lambda j, i: (i, 0))] * 2
    elif epilogue == "conv":
        cj = (col0 - OFF_XBC) // tn
        in_specs += [pl.BlockSpec((SSD_CONV, tn), lambda j, i: (0, cj + j)),
                     pl.BlockSpec((1, tn), lambda j, i: (0, cj + j))]
        scratch.append(pltpu.VMEM((SUBLANES, tn), f32))
    body = functools.partial(_proj_body, epilogue=epilogue, tiles_per_seq=seq // tm,
                             k_first_col_block=(OFF_K - col0) // tn)
    return pl.pallas_call(
        body,
        grid=(ncols // tn, t // tm),
        in_specs=in_specs,
        out_specs=pl.BlockSpec((tm, tn), lambda j, i: (i, j)),
        out_shape=jax.ShapeDtypeStruct((t, ncols), bf16),
        scratch_shapes=scratch,
        compiler_params=_params(2),
        name="proj_" + epilogue,
    )(h, w, *extra)


def _retention_body(q_ref, k_ref, v_ref, g_ref, dmask_ref, qdec_ref, kdec_ref, cdec_ref,
                    o_ref, state_ref):
    blk = pl.program_id(1)
    h0 = pl.program_id(2) * RET_HB

    @pl.when(blk == 0)
    def _():
        for hh in range(RET_HB):
            state_ref[h0 + hh] = jnp.zeros((RET_DK, RET_DV), f32)

    for hh in range(RET_HB):
        h = h0 + hh
        cs = slice(hh * RET_DK, (hh + 1) * RET_DK)
        qb = q_ref[:, cs]
        kb = k_ref[:, cs]
        v = v_ref[:, cs]
        kd = (kb.astype(f32) * kdec_ref[h]).astype(bf16)

        scores = _dot(qb, kb, 1, 1) * dmask_ref[h]
        state = state_ref[h]
        y = _dot(scores.astype(bf16), v)
        y = y + _dot(qb, state.astype(bf16)) * qdec_ref[h]
        state_ref[h] = state * cdec_ref[h] + _dot(kd, v, 0, 0)

        mu = jnp.mean(y, axis=-1, keepdims=True)
        yc = y - mu
        var = jnp.mean(yc * yc, axis=-1, keepdims=True)
        yn = yc * lax.rsqrt(var + EPS)
        o_ref[:, cs] = (yn * g_ref[:, cs].astype(f32)).astype(o_ref.dtype)


def _retention_tables():
    log_gamma = jnp.log1p(-(2.0 ** (-5.0 - jnp.arange(RET_HEADS, dtype=f32))))
    lg = log_gamma[:, None, None]
    n = jnp.arange(RET_BLK, dtype=f32)
    diff = n[:, None] - n[None, :]
    ci = (jnp.arange(RET_BLK) // CHUNK)
    same = ci[:, None] == ci[None, :]
    later = ci[:, None] > ci[None, :]
    expo = jnp.where(same[None], jnp.abs(diff)[None], diff[None]) * lg
    dmask = jnp.where((same | later)[None], jnp.exp(expo), 0.0).astype(f32)
    qdec = jnp.exp((n[None, :, None] + 1.0) * lg)
    kdec = jnp.exp((RET_BLK - 1.0 - n[None, :, None]) * lg)
    cdec = jnp.exp(RET_BLK * lg)
    qdec = jnp.broadcast_to(qdec, (RET_HEADS, RET_BLK, RET_DV))
    kdec = jnp.broadcast_to(kdec, (RET_HEADS, RET_BLK, RET_DK))
    cdec = jnp.broadcast_to(cdec, (RET_HEADS, RET_DK, RET_DV))
    return dmask, qdec, kdec, cdec


def _retention(qk, v, gz, batch, seq):
    t = batch * seq
    nblk = seq // RET_BLK
    assert seq % RET_BLK == 0
    dmask, qdec, kdec, cdec = _retention_tables()

    def rows(b, n, h):
        return b * nblk + n

    const3 = lambda b, n, h: (0, 0, 0)
    hgroups = RET_HEADS // RET_HB
    wide = RET_HB * RET_DK
    return pl.pallas_call(
        _retention_body,
        grid=(batch, nblk, hgroups),
        in_specs=[
            pl.BlockSpec((RET_BLK, wide), lambda b, n, h: (rows(b, n, h), h)),
            pl.BlockSpec((RET_BLK, wide), lambda b, n, h: (rows(b, n, h), hgroups + h)),
            pl.BlockSpec((RET_BLK, wide), lambda b, n, h: (rows(b, n, h), h)),
            pl.BlockSpec((RET_BLK, wide), lambda b, n, h: (rows(b, n, h), h)),
            pl.BlockSpec((RET_HEADS, RET_BLK, RET_BLK), const3),
            pl.BlockSpec((RET_HEADS, RET_BLK, RET_DV), const3),
            pl.BlockSpec((RET_HEADS, RET_BLK, RET_DK), const3),
            pl.BlockSpec((RET_HEADS, RET_DK, RET_DV), const3),
        ],
        out_specs=pl.BlockSpec((RET_BLK, wide), lambda b, n, h: (rows(b, n, h), h)),
        out_shape=jax.ShapeDtypeStruct((t, RET_WIDTH), bf16),
        scratch_shapes=[pltpu.VMEM((RET_HEADS, RET_DK, RET_DV), f32)],
        compiler_params=_params(3),
        name="retention",
    )(qk, qk, v, gz, dmask, qdec, kdec, cdec)


def _split3(a):
    hi = a.astype(bf16).astype(f32)
    r = a - hi
    mid = r.astype(bf16).astype(f32)
    lo = (r - mid).astype(bf16).astype(f32)
    return [hi, mid, lo]


SSD_KGRP = 32
SSD_K_ACUM, SSD_K_ONES, SSD_K_EEXP, SSD_K_TAIL = 0, 32, 64, 96


def _group32(parts):
    return jnp.concatenate(list(parts) + [jnp.zeros_like(parts[0])], axis=0)


def _ssd_decay_prep(dt_raw, bias, neg_a, upper, blockdiag):
    H, L = dt_raw.shape
    raw = dt_raw + bias
    dt = jnp.maximum(raw, 0.0) + jnp.log1p(jnp.exp(-jnp.abs(raw)))
    csum = _dot(jnp.concatenate(_split3(dt * neg_a), axis=0).astype(bf16), upper)
    acum = (csum[0:H] + csum[H:2 * H]) + csum[2 * H:3 * H]
    alast = acum[:, L - 1:L]
    tailw = jnp.exp(alast - acum) * dt
    acum2 = acum * LOG2E
    srow = acum2 - jnp.maximum(jnp.log(dt) * LOG2E, -200.0)
    ones = jnp.ones_like(acum2)
    stack = jnp.concatenate([_group32(_split3(acum2)), _group32([ones, ones, ones]),
                             _group32(_split3(jnp.exp2(acum2))), _group32(_split3(tailw))], axis=0)
    lhs = stack.T.astype(bf16)
    dyn = jnp.tile(_group32([-p for p in _split3(srow)]), (1, H))
    dyn = jnp.where(blockdiag, dyn, 0.0).astype(bf16)
    return lhs, dyn


def _ssd_body(xs_ref, b_ref, c_ref, z_ref, dt_ref, dtn_ref, bias_ref, alog_ref, dexp_ref,
              rconst_ref, rexp_ref, o_ref, state_ref, lhs_ref, rhs_ref):
    blk = pl.program_id(2)
    tb = xs_ref.shape[0]
    L = SSD_L
    nsb = tb // L
    H = SSD_HPG
    GW = SSD_GW

    neg_a = -jnp.exp(alog_ref[...])
    bias = bias_ref[...]
    row = lax.broadcasted_iota(jnp.int32, (L, L), 0)
    col = lax.broadcasted_iota(jnp.int32, (L, L), 1)
    causal = row >= col
    upper = jnp.where(row <= col, 1.0, 0.0).astype(bf16)
    lane_lo = lax.broadcasted_iota(jnp.int32, (L, LANES), 1) < SSD_HEADDIM
    lane_lo2 = jnp.concatenate([lane_lo, lane_lo], axis=0)
    row_top = lax.broadcasted_iota(jnp.int32, (2 * L, LANES), 0) < L
    keep_bd = row_top == lane_lo2
    krow = lax.broadcasted_iota(jnp.int32, (SSD_KGRP, H * L), 0)
    kcol = lax.broadcasted_iota(jnp.int32, (SSD_KGRP, H * L), 1)
    blockdiag = (krow % H) == (kcol // L)

    def prep(dt_raw):
        return _ssd_decay_prep(dt_raw, bias, neg_a, upper, blockdiag)

    @pl.when(blk == 0)
    def _():
        state_ref[...] = jnp.zeros(state_ref.shape, f32)
        for sb in range(nsb):
            lhs0, dyn0 = prep(dt_ref[:, sb * L:(sb + 1) * L])
            lhs_ref[sb] = lhs0
            rhs_ref[sb] = rconst_ref[...]
            rhs_ref[sb, SSD_K_ONES:SSD_K_ONES + SSD_KGRP, :] = dyn0

    for sb in range(nsb):
        t0 = sb * L
        lhs = lhs_ref[sb]
        segall = _dot(lhs, rhs_ref[sb])
        expall = _dot(lhs, rexp_ref[...])
        lhs_next, dyn_next = prep(dtn_ref[:, t0:t0 + L])
        xcb = xs_ref[t0:t0 + L, :]
        bcb = b_ref[t0:t0 + L, :]
        ccb = c_ref[t0:t0 + L, :]
        xc = xcb.astype(f32)

        cb = _dot(ccb, bcb, 1, 1)

        ypieces = []
        for j in range(SSD_HPG // 2):
            gs = []
            for hh in (2 * j, 2 * j + 1):
                seg = segall[:, hh * L:(hh + 1) * L]
                gs.append((cb * jnp.exp2(jnp.where(causal, seg, -jnp.inf))).astype(bf16))
            gpair = jnp.concatenate(gs, axis=1)
            xpair = xcb[:, j * LANES:(j + 1) * LANES]
            x2 = jnp.concatenate([xpair, xpair], axis=0)
            xbd = jnp.where(keep_bd, x2, jnp.zeros_like(x2))
            ypieces.append(_dot(gpair, xbd))
        y = jnp.concatenate(ypieces, axis=-1)

        state = state_ref[...]
        ycross = _dot(ccb, state.astype(bf16))
        y = y + ycross * expall[:, :GW]
        y = y + dexp_ref[...] * xc

        xw = (xc * expall[:, GW:]).astype(bf16)
        sdec = expall[L - 1:L, :GW]
        state_ref[...] = state * sdec + _dot(bcb, xw, 0, 0)

        o_ref[t0:t0 + L, :] = (y * z_ref[t0:t0 + L, :].astype(f32)).astype(o_ref.dtype)
        lhs_ref[sb] = lhs_next
        rhs_ref[sb, SSD_K_ONES:SSD_K_ONES + SSD_KGRP, :] = dyn_next


def _ssd(xbc, gz, dt_t, dt_bias, a_log, d_skip, batch, seq):
    t = batch * seq
    tb = min(SSD_TB, seq)
    nblk = seq // tb
    assert seq % tb == 0 and tb % SSD_L == 0
    L = SSD_L
    bias_b = jnp.broadcast_to(dt_bias.astype(f32)[:, None], (SSD_HEADS, L))
    alog_b = jnp.broadcast_to(a_log.astype(f32)[:, None], (SSD_HEADS, L))
    d_exp = jnp.repeat(d_skip.astype(f32), SSD_HEADDIM).reshape(1, SSD_WIDTH)

    kk = np.arange(4 * SSD_KGRP)
    k_head = kk % SSD_HPG
    k_live = (kk % SSD_KGRP) < 3 * SSD_HPG
    k_grp = kk // SSD_KGRP
    seg_lane_head = np.arange(SSD_HPG * L) // L
    rconst = ((k_grp == SSD_K_ACUM // SSD_KGRP) & k_live)[:, None] & (k_head[:, None] == seg_lane_head[None, :])
    x_lane_head = np.arange(SSD_GW) // SSD_HEADDIM
    hit = k_live[:, None] & (k_head[:, None] == x_lane_head[None, :])
    rexp = np.concatenate([hit & (k_grp == SSD_K_EEXP // SSD_KGRP)[:, None],
                           hit & (k_grp == SSD_K_TAIL // SSD_KGRP)[:, None]], axis=1)
    rconst = jnp.asarray(rconst, dtype=bf16)
    rexp = jnp.asarray(rexp, dtype=bf16)

    cz = RET_WIDTH // SSD_GW
    cbm = SSD_WIDTH // SSD_STATE
    ccm = cbm + SSD_GROUPS

    def rows(b, g, n):
        return b * nblk + n

    def next_rows(b, g, n):
        return jnp.minimum(b * nblk + n + 1, batch * nblk - 1)

    return pl.pallas_call(
        _ssd_body,
        grid=(batch, SSD_GROUPS, nblk),
        in_specs=[
            pl.BlockSpec((tb, SSD_GW), lambda b, g, n: (rows(b, g, n), g)),
            pl.BlockSpec((tb, SSD_STATE), lambda b, g, n: (rows(b, g, n), cbm + g)),
            pl.BlockSpec((tb, SSD_STATE), lambda b, g, n: (rows(b, g, n), ccm + g)),
            pl.BlockSpec((tb, SSD_GW), lambda b, g, n: (rows(b, g, n), cz + g)),
            pl.BlockSpec((SSD_HPG, tb), lambda b, g, n: (g, rows(b, g, n))),
            pl.BlockSpec((SSD_HPG, tb), lambda b, g, n: (g, next_rows(b, g, n))),
            pl.BlockSpec((SSD_HPG, L), lambda b, g, n: (g, 0)),
            pl.BlockSpec((SSD_HPG, L), lambda b, g, n: (g, 0)),
            pl.BlockSpec((1, SSD_GW), lambda b, g, n: (0, g)),
            pl.BlockSpec((4 * SSD_KGRP, SSD_HPG * L), lambda b, g, n: (0, 0)),
            pl.BlockSpec((4 * SSD_KGRP, 2 * SSD_GW), lambda b, g, n: (0, 0)),
        ],
        out_specs=pl.BlockSpec((tb, SSD_GW), lambda b, g, n: (rows(b, g, n), g)),
        out_shape=jax.ShapeDtypeStruct((t, SSD_WIDTH), bf16),
        scratch_shapes=[pltpu.VMEM((SSD_STATE, SSD_GW), f32),
                        pltpu.VMEM((tb // L, L, 4 * SSD_KGRP), bf16),
                        pltpu.VMEM((tb // L, 4 * SSD_KGRP, SSD_HPG * L), bf16)],
        compiler_params=_params(3),
        name="ssd",
    )(xbc, xbc, xbc, gz, dt_t, dt_t, bias_b, alog_b, d_exp, rconst, rexp)


def _merge_body(yr_ref, ys_ref, gr_ref, gs_ref, nw_ref, wr_ref, ws_ref, o_ref):
    ys = ys_ref[...].astype(f32)
    ms = jnp.mean(ys * ys, axis=-1, keepdims=True)
    ysn = (ys * lax.rsqrt(ms + EPS) * nw_ref[...]).astype(bf16)
    p_r = _dot(yr_ref[...], wr_ref[...])
    p_s = _dot(ysn, ws_ref[...])
    merged = gr_ref[...].astype(f32) * p_r + gs_ref[...].astype(f32) * p_s
    o_ref[...] = merged.astype(o_ref.dtype)


def _merge(y_r, y_s, gates, ssd_norm_w, w_r, w_s):
    t = y_r.shape[0]
    tm = min(TM_MERGE, t)
    assert t % tm == 0
    const = lambda i: (0, 0)
    return pl.pallas_call(
        _merge_body,
        grid=(t // tm,),
        in_specs=[
            pl.BlockSpec((tm, RET_WIDTH), lambda i: (i, 0)),
            pl.BlockSpec((tm, SSD_WIDTH), lambda i: (i, 0)),
            pl.BlockSpec((tm, D_MODEL), lambda i: (i, 0)),
            pl.BlockSpec((tm, D_MODEL), lambda i: (i, 1)),
            pl.BlockSpec((1, SSD_WIDTH), const),
            pl.BlockSpec((RET_WIDTH, D_MODEL), const, pipeline_mode=pl.Buffered(1)),
            pl.BlockSpec((SSD_WIDTH, D_MODEL), const, pipeline_mode=pl.Buffered(1)),
        ],
        out_specs=pl.BlockSpec((tm, D_MODEL), lambda i: (i, 0)),
        out_shape=jax.ShapeDtypeStruct((t, D_MODEL), bf16),
        compiler_params=_params(1),
        name="merge",
    )(y_r, y_s, gates, gates, ssd_norm_w, w_r, w_s)


def _out_body(x_ref, m_ref, wo_ref, nw_ref, o_ref):
    xo = x_ref[...] + _dot(m_ref[...], wo_ref[...])
    ms = jnp.mean(xo * xo, axis=-1, keepdims=True)
    o_ref[...] = xo * lax.rsqrt(ms + EPS) * nw_ref[...]


def _out_proj(x2, merged, w_o, norm_f_w):
    t = x2.shape[0]
    tm = min(TM_OUT, t)
    assert t % tm == 0
    const = lambda i: (0, 0)
    return pl.pallas_call(
        _out_body,
        grid=(t // tm,),
        in_specs=[
            pl.BlockSpec((tm, D_MODEL), lambda i: (i, 0)),
            pl.BlockSpec((tm, D_MODEL), lambda i: (i, 0)),
            pl.BlockSpec((D_MODEL, D_MODEL), const, pipeline_mode=pl.Buffered(1)),
            pl.BlockSpec((1, D_MODEL), const),
        ],
        out_specs=pl.BlockSpec((tm, D_MODEL), lambda i: (i, 0)),
        out_shape=jax.ShapeDtypeStruct((t, D_MODEL), f32),
        compiler_params=_params(1),
        name="out_proj",
    )(x2, merged, w_o, norm_f_w)


def kernel(x, positions, norm1_w, w_in, conv_w, conv_b, dt_bias, a_log, d_skip,
           ssd_norm_w, w_br_ret, w_br_ssd, w_out, norm_f_w):
    batch, seq, d = x.shape
    assert d == D_MODEL and w_in.shape[0] == 1
    t = batch * seq
    x2 = x.reshape(t, d)
    w = w_in

    w_dt = jnp.pad(w_in[0, :, OFF_DT:OFF_DT + SSD_HEADS],
                   ((0, 0), (0, DT_PAD - SSD_HEADS))).astype(bf16)
    w_gates = w_in[:, :, OFF_GATES:]
    pos_b = jnp.broadcast_to(positions.astype(f32).reshape(t, 1), (t, ROPE_HALF))
    inv_freq = (ROPE_THETA ** (-jnp.arange(ROPE_HALF, dtype=f32) / ROPE_HALF)).reshape(1, ROPE_HALF)

    h, dt_t, cos, sin = _prenorm(x2, norm1_w[0].reshape(1, d), w_dt, pos_b, inv_freq)
    qk = _proj(h, w, OFF_Q, 2 * RET_WIDTH, "rope", seq, (cos, sin))
    v = _proj(h, w, OFF_V, RET_WIDTH, "none", seq)
    gz = _proj(h, w, OFF_G, RET_WIDTH + SSD_WIDTH, "silu", seq)
    xbc = _proj(h, w, OFF_XBC, SSD_CONV_DIM, "conv", seq,
                (conv_w[0].astype(f32), conv_b[0].astype(f32).reshape(1, SSD_CONV_DIM)))
    gates = _proj(h, w_gates, 0, 2 * D_MODEL, "sigmoid", seq)

    y_r = _retention(qk, v, gz, batch, seq)
    y_s = _ssd(xbc, gz, dt_t, dt_bias[0], a_log[0], d_skip[0], batch, seq)
    merged = _merge(y_r, y_s, gates, ssd_norm_w[0].reshape(1, SSD_WIDTH),
                    w_br_ret[0].astype(bf16), w_br_ssd[0].astype(bf16))
    out = _out_proj(x2, merged, w_out[0].astype(bf16), norm_f_w.reshape(1, d))
    return out.reshape(batch, seq, d)
```

```python
import functools
import math

import numpy as np
import jax
import jax.numpy as jnp
from jax import lax
from jax.experimental import pallas as pl
from jax.experimental.pallas import tpu as pltpu

f32 = jnp.float32
bf16 = jnp.bfloat16

D_MODEL = 2048
CHUNK = 64
EPS = 1e-6

RET_HEADS = 8
RET_DK = 256
RET_DV = 256
RET_WIDTH = RET_HEADS * RET_DV
ROPE_THETA = 10000.0
ROPE_HALF = RET_DK // 2

SSD_WIDTH = 2 * D_MODEL
SSD_HEADDIM = 64
SSD_HEADS = SSD_WIDTH // SSD_HEADDIM
SSD_GROUPS = 8
SSD_HPG = SSD_HEADS // SSD_GROUPS
SSD_STATE = 128
SSD_CONV = 4
SSD_GW = SSD_HPG * SSD_HEADDIM
SSD_CONV_DIM = SSD_WIDTH + 2 * SSD_GROUPS * SSD_STATE

OFF_Q = 0
OFF_K = OFF_Q + RET_WIDTH
OFF_V = OFF_K + RET_WIDTH
OFF_G = OFF_V + RET_WIDTH
OFF_Z = OFF_G + RET_WIDTH
OFF_XBC = OFF_Z + SSD_WIDTH
OFF_DT = OFF_XBC + SSD_CONV_DIM
OFF_GATES = OFF_DT + SSD_HEADS
DT_PAD = 128

LANES = 128
SUBLANES = 8
VMEM_LIMIT = 56 * 1024 * 1024
LOG2E = 1.0 / math.log(2.0)

TM_PRE = 1024
TM_IN = 2048
TN_IN = 1024
SLAB = 256
RET_BLK = 256
RET_HB = 8
SSD_TB = 1024
SSD_L = 128
TM_MERGE = 256
TM_OUT = 512


def _sigmoid(x):
    return 1.0 / (1.0 + jnp.exp2(x * (-LOG2E)))


def _silu(x):
    return x * _sigmoid(x)


def _dot(a, b, ca=1, cb=0):
    return lax.dot_general(a, b, (((ca,), (cb,)), ((), ())), preferred_element_type=f32)


def _params(n_axes):
    return pltpu.CompilerParams(dimension_semantics=("arbitrary",) * n_axes,
                                vmem_limit_bytes=VMEM_LIMIT)


def _prenorm_body(x_ref, nw_ref, wdt_ref, pos_ref, freq_ref, h_ref, dt_ref, cos_ref, sin_ref):
    x = x_ref[...]
    ms = jnp.mean(x * x, axis=-1, keepdims=True)
    h = (x * lax.rsqrt(ms + EPS) * nw_ref[...]).astype(bf16)
    h_ref[...] = h
    dt_ref[...] = _dot(wdt_ref[...].astype(bf16), h, 1, 1)
    ang = pos_ref[...] * freq_ref[...]
    cos_ref[...] = jnp.cos(ang)
    sin_ref[...] = jnp.sin(ang)


def _prenorm(x2, norm_w, w_dt, pos_b, inv_freq):
    t = x2.shape[0]
    tm = min(TM_PRE, t)
    assert t % tm == 0
    const = lambda i: (0, 0)
    return pl.pallas_call(
        _prenorm_body,
        grid=(t // tm,),
        in_specs=[
            pl.BlockSpec((tm, D_MODEL), lambda i: (i, 0)),
            pl.BlockSpec((1, D_MODEL), const),
            pl.BlockSpec((None, DT_PAD, D_MODEL), lambda i: (0, OFF_DT // DT_PAD, 0)),
            pl.BlockSpec((tm, ROPE_HALF), lambda i: (i, 0)),
            pl.BlockSpec((1, ROPE_HALF), const),
        ],
        out_specs=[
            pl.BlockSpec((tm, D_MODEL), lambda i: (i, 0)),
            pl.BlockSpec((DT_PAD, tm), lambda i: (0, i)),
            pl.BlockSpec((tm, ROPE_HALF), lambda i: (i, 0)),
            pl.BlockSpec((tm, ROPE_HALF), lambda i: (i, 0)),
        ],
        out_shape=[
            jax.ShapeDtypeStruct((t, D_MODEL), bf16),
            jax.ShapeDtypeStruct((DT_PAD, t), f32),
            jax.ShapeDtypeStruct((t, ROPE_HALF), f32),
            jax.ShapeDtypeStruct((t, ROPE_HALF), f32),
        ],
        compiler_params=_params(1),
        name="prenorm",
    )(x2, norm_w, w_dt, pos_b, inv_freq)


def _proj_body(*refs, epilogue, tiles_per_seq, k_first_col_block):
    if epilogue == "rope":
        h_ref, w_ref, cos_ref, sin_ref, o_ref, wb_ref = refs
    elif epilogue == "conv":
        h_ref, w_ref, cw_ref, cb_ref, o_ref, wb_ref, halo_ref = refs
    else:
        h_ref, w_ref, o_ref, wb_ref = refs
    j = pl.program_id(0)
    i = pl.program_id(1)
    tm, tn = o_ref.shape

    @pl.when(i == 0)
    def _():
        for c in range(0, D_MODEL, SLAB):
            wb_ref[c:c + SLAB, :] = w_ref[:, c:c + SLAB].T.astype(bf16)

    if epilogue == "conv":
        @pl.when(i % tiles_per_seq == 0)
        def _():
            halo_ref[...] = jnp.zeros((SUBLANES, tn), f32)

    if epilogue == "rope":
        scale = jnp.where(j >= k_first_col_block, RET_DK ** -0.5, 1.0).astype(f32)
        cos = cos_ref[...] * scale
        sin = sin_ref[...] * scale

    for n in range(tn // SLAB):
        c0 = n * SLAB
        acc = _dot(h_ref[...], wb_ref[:, c0:c0 + SLAB])
        if epilogue == "none":
            out = acc
        elif epilogue == "silu":
            out = _silu(acc)
        elif epilogue == "sigmoid":
            out = _sigmoid(acc)
        elif epilogue == "rope":
            t1 = acc[:, :ROPE_HALF]
            t2 = acc[:, ROPE_HALF:]
            out = jnp.concatenate([t1 * cos - t2 * sin, t2 * cos + t1 * sin], axis=-1)
        elif epilogue == "conv":
            ext = jnp.concatenate([halo_ref[:, c0:c0 + SLAB], acc], axis=0)
            w = cw_ref[:, c0:c0 + SLAB]
            x1 = pltpu.roll(ext, 1, axis=0)
            a2 = pltpu.roll(ext * w[1:2] + x1 * w[0:1], 2, axis=0)
            out = _silu((ext * w[3:4] + x1 * w[2:3] + a2 + cb_ref[:, c0:c0 + SLAB])[SUBLANES:])
            halo_ref[:, c0:c0 + SLAB] = acc[tm - SUBLANES:tm]
        o_ref[:, c0:c0 + SLAB] = out.astype(o_ref.dtype)


def _proj(h, w, col0, ncols, epilogue, seq, extra=()):
    t = h.shape[0]
    tm = min(TM_IN, t, seq)
    tn = TN_IN
    assert t % tm == 0 and seq % tm == 0 and ncols % tn == 0 and col0 % tn == 0
    jb = col0 // tn
    in_specs = [
        pl.BlockSpec((tm, D_MODEL), lambda j, i: (i, 0)),
        pl.BlockSpec((None, tn, D_MODEL), lambda j, i: (0, jb + j, 0)),
    ]
    scratch = [pltpu.VMEM((D_MODEL, tn), bf16)]
    if epilogue == "rope":
        in_specs += [pl.BlockSpec((tm, ROPE_HALF), lambda j, i: (i, 0))] * 2
    elif epilogue == "conv":
        cj = (col0 - OFF_XBC) // tn
        in_specs += [pl.BlockSpec((SSD_CONV, tn), lambda j, i: (0, cj + j)),
                     pl.BlockSpec((1, tn), lambda j, i: (0, cj + j))]
        scratch.append(pltpu.VMEM((SUBLANES, tn), f32))
    body = functools.partial(_proj_body, epilogue=epilogue, tiles_per_seq=seq // tm,
                             k_first_col_block=(OFF_K - col0) // tn)
    return pl.pallas_call(
        body,
        grid=(ncols // tn, t // tm),
        in_specs=in_specs,
        out_specs=pl.BlockSpec((tm, tn), lambda j, i: (i, j)),
        out_shape=jax.ShapeDtypeStruct((t, ncols), bf16),
        scratch_shapes=scratch,
        compiler_params=_params(2),
        name="proj_" + epilogue,
    )(h, w, *extra)


def _retention_body(q_ref, k_ref, v_ref, g_ref, dmask_ref, qdec_ref, kdec_ref, cdec_ref,
                    o_ref, state_ref):
    blk = pl.program_id(1)
    h0 = pl.program_id(2) * RET_HB

    @pl.when(blk == 0)
    def _():
        for hh in range(RET_HB):
            state_ref[h0 + hh] = jnp.zeros((RET_DK, RET_DV), f32)

    for hh in range(RET_HB):
        h = h0 + hh
        cs = slice(hh * RET_DK, (hh + 1) * RET_DK)
        qb = q_ref[:, cs]
        kb = k_ref[:, cs]
        v = v_ref[:, cs]
        kd = (kb.astype(f32) * kdec_ref[h]).astype(bf16)

        scores = _dot(qb, kb, 1, 1) * dmask_ref[h]
        state = state_ref[h]
        y = _dot(scores.astype(bf16), v)
        y = y + _dot(qb, state.astype(bf16)) * qdec_ref[h]
        state_ref[h] = state * cdec_ref[h] + _dot(kd, v, 0, 0)

        mu = jnp.mean(y, axis=-1, keepdims=True)
        yc = y - mu
        var = jnp.mean(yc * yc, axis=-1, keepdims=True)
        yn = yc * lax.rsqrt(var + EPS)
        o_ref[:, cs] = (yn * g_ref[:, cs].astype(f32)).astype(o_ref.dtype)


def _retention_tables():
    log_gamma = jnp.log1p(-(2.0 ** (-5.0 - jnp.arange(RET_HEADS, dtype=f32))))
    lg = log_gamma[:, None, None]
    n = jnp.arange(RET_BLK, dtype=f32)
    diff = n[:, None] - n[None, :]
    ci = (jnp.arange(RET_BLK) // CHUNK)
    same = ci[:, None] == ci[None, :]
    later = ci[:, None] > ci[None, :]
    expo = jnp.where(same[None], jnp.abs(diff)[None], diff[None]) * lg
    dmask = jnp.where((same | later)[None], jnp.exp(expo), 0.0).astype(f32)
    qdec = jnp.exp((n[None, :, None] + 1.0) * lg)
    kdec = jnp.exp((RET_BLK - 1.0 - n[None, :, None]) * lg)
    cdec = jnp.exp(RET_BLK * lg)
    qdec = jnp.broadcast_to(qdec, (RET_HEADS, RET_BLK, RET_DV))
    kdec = jnp.broadcast_to(kdec, (RET_HEADS, RET_BLK, RET_DK))
    cdec = jnp.broadcast_to(cdec, (RET_HEADS, RET_DK, RET_DV))
    return dmask, qdec, kdec, cdec


def _retention(qk, v, gz, batch, seq):
    t = batch * seq
    nblk = seq // RET_BLK
    assert seq % RET_BLK == 0
    dmask, qdec, kdec, cdec = _retention_tables()

    def rows(b, n, h):
        return b * nblk + n

    const3 = lambda b, n, h: (0, 0, 0)
    hgroups = RET_HEADS // RET_HB
    wide = RET_HB * RET_DK
    return pl.pallas_call(
        _retention_body,
        grid=(batch, nblk, hgroups),
        in_specs=[
            pl.BlockSpec((RET_BLK, wide), lambda b, n, h: (rows(b, n, h), h)),
            pl.BlockSpec((RET_BLK, wide), lambda b, n, h: (rows(b, n, h), hgroups + h)),
            pl.BlockSpec((RET_BLK, wide), lambda b, n, h: (rows(b, n, h), h)),
            pl.BlockSpec((RET_BLK, wide), lambda b, n, h: (rows(b, n, h), h)),
            pl.BlockSpec((RET_HEADS, RET_BLK, RET_BLK), const3),
            pl.BlockSpec((RET_HEADS, RET_BLK, RET_DV), const3),
            pl.BlockSpec((RET_HEADS, RET_BLK, RET_DK), const3),
            pl.BlockSpec((RET_HEADS, RET_DK, RET_DV), const3),
        ],
        out_specs=pl.BlockSpec((RET_BLK, wide), lambda b, n, h: (rows(b, n, h), h)),
        out_shape=jax.ShapeDtypeStruct((t, RET_WIDTH), bf16),
        scratch_shapes=[pltpu.VMEM((RET_HEADS, RET_DK, RET_DV), f32)],
        compiler_params=_params(3),
        name="retention",
    )(qk, qk, v, gz, dmask, qdec, kdec, cdec)


def _split3(a):
    hi = a.astype(bf16).astype(f32)
    r = a - hi
    mid = r.astype(bf16).astype(f32)
    lo = (r - mid).astype(bf16).astype(f32)
    return [hi, mid, lo]


SSD_KGRP = 32
SSD_K_ACUM, SSD_K_ONES, SSD_K_EEXP, SSD_K_TAIL = 0, 32, 64, 96


def _group32(parts):
    return jnp.concatenate(list(parts) + [jnp.zeros_like(parts[0])], axis=0)


def _ssd_decay_prep(dt_raw, bias, neg_a, upper, blockdiag):
    H, L = dt_raw.shape
    raw = dt_raw + bias
    dt = jnp.maximum(raw, 0.0) + jnp.log1p(jnp.exp(-jnp.abs(raw)))
    csum = _dot(jnp.concatenate(_split3(dt * neg_a), axis=0).astype(bf16), upper)
    acum = (csum[0:H] + csum[H:2 * H]) + csum[2 * H:3 * H]
    alast = acum[:, L - 1:L]
    tailw = jnp.exp(alast - acum) * dt
    acum2 = acum * LOG2E
    srow = acum2 - jnp.maximum(jnp.log(dt) * LOG2E, -200.0)
    ones = jnp.ones_like(acum2)
    stack = jnp.concatenate([_group32(_split3(acum2)), _group32([ones, ones, ones]),
                             _group32(_split3(jnp.exp2(acum2))), _group32(_split3(tailw))], axis=0)
    lhs = stack.T.astype(bf16)
    dyn = jnp.tile(_group32([-p for p in _split3(srow)]), (1, H))
    dyn = jnp.where(blockdiag, dyn, 0.0).astype(bf16)
    return lhs, dyn


def _ssd_body(xs_ref, b_ref, c_ref, z_ref, dt_ref, dtn_ref, bias_ref, alog_ref, dexp_ref,
              rconst_ref, rexp_ref, o_ref, state_ref, lhs_ref, rhs_ref):
    blk = pl.program_id(2)
    tb = xs_ref.shape[0]
    L = SSD_L
    nsb = tb // L
    H = SSD_HPG
    GW = SSD_GW

    neg_a = -jnp.exp(alog_ref[...])
    bias = bias_ref[...]
    row = lax.broadcasted_iota(jnp.int32, (L, L), 0)
    col = lax.broadcasted_iota(jnp.int32, (L, L), 1)
    causal = row >= col
    upper = jnp.where(row <= col, 1.0, 0.0).astype(bf16)
    lane_lo = lax.broadcasted_iota(jnp.int32, (L, LANES), 1) < SSD_HEADDIM
    lane_lo2 = jnp.concatenate([lane_lo, lane_lo], axis=0)
    row_top = lax.broadcasted_iota(jnp.int32, (2 * L, LANES), 0) < L
    keep_bd = row_top == lane_lo2
    krow = lax.broadcasted_iota(jnp.int32, (SSD_KGRP, H * L), 0)
    kcol = lax.broadcasted_iota(jnp.int32, (SSD_KGRP, H * L), 1)
    blockdiag = (krow % H) == (kcol // L)

    def prep(dt_raw):
        return _ssd_decay_prep(dt_raw, bias, neg_a, upper, blockdiag)

    @pl.when(blk == 0)
    def _():
        state_ref[...] = jnp.zeros(state_ref.shape, f32)
        for sb in range(nsb):
            lhs0, dyn0 = prep(dt_ref[:, sb * L:(sb + 1) * L])
            lhs_ref[sb] = lhs0
            rhs_ref[sb] = rconst_ref[...]
            rhs_ref[sb, SSD_K_ONES:SSD_K_ONES + SSD_KGRP, :] = dyn0

    for sb in range(nsb):
        t0 = sb * L
        lhs = lhs_ref[sb]
        segall = _dot(lhs, rhs_ref[sb])
        expall = _dot(lhs, rexp_ref[...])
        lhs_next, dyn_next = prep(dtn_ref[:, t0:t0 + L])
        xcb = xs_ref[t0:t0 + L, :]
        bcb = b_ref[t0:t0 + L, :]
        ccb = c_ref[t0:t0 + L, :]
        xc = xcb.astype(f32)

        cb = _dot(ccb, bcb, 1, 1)

        ypieces = []
        for j in range(SSD_HPG // 2):
            gs = []
            for hh in (2 * j, 2 * j + 1):
                seg = segall[:, hh * L:(hh + 1) * L]
                gs.append((cb * jnp.exp2(jnp.where(causal, seg, -jnp.inf))).astype(bf16))
            gpair = jnp.concatenate(gs, axis=1)
            xpair = xcb[:, j * LANES:(j + 1) * LANES]
            x2 = jnp.concatenate([xpair, xpair], axis=0)
            xbd = jnp.where(keep_bd, x2, jnp.zeros_like(x2))
            ypieces.append(_dot(gpair, xbd))
        y = jnp.concatenate(ypieces, axis=-1)

        state = state_ref[...]
        ycross = _dot(ccb, state.astype(bf16))
        y = y + ycross * expall[:, :GW]
        y = y + dexp_ref[...] * xc

        xw = (xc * expall[:, GW:]).astype(bf16)
        sdec = expall[L - 1:L, :GW]
        state_ref[...] = state * sdec + _dot(bcb, xw, 0, 0)

        o_ref[t0:t0 + L, :] = (y * z_ref[t0:t0 + L, :].astype(f32)).astype(o_ref.dtype)
        lhs_ref[sb] = lhs_next
        rhs_ref[sb, SSD_K_ONES:SSD_K_ONES + SSD_KGRP, :] = dyn_next


def _ssd(xbc, gz, dt_t, dt_bias, a_log, d_skip, batch, seq):
    t = batch * seq
    tb = min(SSD_TB, seq)
    nblk = seq // tb
    assert seq % tb == 0 and tb % SSD_L == 0
    L = SSD_L
    bias_b = jnp.broadcast_to(dt_bias.astype(f32)[:, None], (SSD_HEADS, L))
    alog_b = jnp.broadcast_to(a_log.astype(f32)[:, None], (SSD_HEADS, L))
    d_exp = jnp.repeat(d_skip.astype(f32), SSD_HEADDIM).reshape(1, SSD_WIDTH)

    kk = np.arange(4 * SSD_KGRP)
    k_head = kk % SSD_HPG
    k_live = (kk % SSD_KGRP) < 3 * SSD_HPG
    k_grp = kk // SSD_KGRP
    seg_lane_head = np.arange(SSD_HPG * L) // L
    rconst = ((k_grp == SSD_K_ACUM // SSD_KGRP) & k_live)[:, None] & (k_head[:, None] == seg_lane_head[None, :])
    x_lane_head = np.arange(SSD_GW) // SSD_HEADDIM
    hit = k_live[:, None] & (k_head[:, None] == x_lane_head[None, :])
    rexp = np.concatenate([hit & (k_grp == SSD_K_EEXP // SSD_KGRP)[:, None],
                           hit & (k_grp == SSD_K_TAIL // SSD_KGRP)[:, None]], axis=1)
    rconst = jnp.asarray(rconst, dtype=bf16)
    rexp = jnp.asarray(rexp, dtype=bf16)

    cz = RET_WIDTH // SSD_GW
    cbm = SSD_WIDTH // SSD_STATE
    ccm = cbm + SSD_GROUPS

    def rows(b, g, n):
        return b * nblk + n

    def next_rows(b, g, n):
        return jnp.minimum(b * nblk + n + 1, batch * nblk - 1)

    return pl.pallas_call(
        _ssd_body,
        grid=(batch, SSD_GROUPS, nblk),
        in_specs=[
            pl.BlockSpec((tb, SSD_GW), lambda b, g, n: (rows(b, g, n), g)),
            pl.BlockSpec((tb, SSD_STATE), lambda b, g, n: (rows(b, g, n), cbm + g)),
            pl.BlockSpec((tb, SSD_STATE), lambda b, g, n: (rows(b, g, n), ccm + g)),
            pl.BlockSpec((tb, SSD_GW), lambda b, g, n: (rows(b, g, n), cz + g)),
            pl.BlockSpec((SSD_HPG, tb), lambda b, g, n: (g, rows(b, g, n))),
            pl.BlockSpec((SSD_HPG, tb), lambda b, g, n: (g, next_rows(b, g, n))),
            pl.BlockSpec((SSD_HPG, L), lambda b, g, n: (g, 0)),
            pl.BlockSpec((SSD_HPG, L), lambda b, g, n: (g, 0)),
            pl.BlockSpec((1, SSD_GW), lambda b, g, n: (0, g)),
            pl.BlockSpec((4 * SSD_KGRP, SSD_HPG * L), lambda b, g, n: (0, 0)),
            pl.BlockSpec((4 * SSD_KGRP, 2 * SSD_GW), lambda b, g, n: (0, 0)),
        ],
        out_specs=pl.BlockSpec((tb, SSD_GW), lambda b, g, n: (rows(b, g, n), g)),
        out_shape=jax.ShapeDtypeStruct((t, SSD_WIDTH), bf16),
        scratch_shapes=[pltpu.VMEM((SSD_STATE, SSD_GW), f32),
                        pltpu.VMEM((tb // L, L, 4 * SSD_KGRP), bf16),
                        pltpu.VMEM((tb // L, 4 * SSD_KGRP, SSD_HPG * L), bf16)],
        compiler_params=_params(3),
        name="ssd",
    )(xbc, xbc, xbc, gz, dt_t, dt_t, bias_b, alog_b, d_exp, rconst, rexp)


def _merge_body(yr_ref, ys_ref, gr_ref, gs_ref, nw_ref, wr_ref, ws_ref, o_ref):
    ys = ys_ref[...].astype(f32)
    ms = jnp.mean(ys * ys, axis=-1, keepdims=True)
    ysn = (ys * lax.rsqrt(ms + EPS) * nw_ref[...]).astype(bf16)
    p_r = _dot(yr_ref[...], wr_ref[...])
    p_s = _dot(ysn, ws_ref[...])
    merged = gr_ref[...].astype(f32) * p_r + gs_ref[...].astype(f32) * p_s
    o_ref[...] = merged.astype(o_ref.dtype)


def _merge(y_r, y_s, gates, ssd_norm_w, w_r, w_s):
    t = y_r.shape[0]
    tm = min(TM_MERGE, t)
    assert t % tm == 0
    const = lambda i: (0, 0)
    return pl.pallas_call(
        _merge_body,
        grid=(t // tm,),
        in_specs=[
            pl.BlockSpec((tm, RET_WIDTH), lambda i: (i, 0)),
            pl.BlockSpec((tm, SSD_WIDTH), lambda i: (i, 0)),
            pl.BlockSpec((tm, D_MODEL), lambda i: (i, 0)),
            pl.BlockSpec((tm, D_MODEL), lambda i: (i, 1)),
            pl.BlockSpec((1, SSD_WIDTH), const),
            pl.BlockSpec((RET_WIDTH, D_MODEL), const, pipeline_mode=pl.Buffered(1)),
            pl.BlockSpec((SSD_WIDTH, D_MODEL), const, pipeline_mode=pl.Buffered(1)),
        ],
        out_specs=pl.BlockSpec((tm, D_MODEL), lambda i: (i, 0)),
        out_shape=jax.ShapeDtypeStruct((t, D_MODEL), bf16),
        compiler_params=_params(1),
        name="merge",
    )(y_r, y_s, gates, gates, ssd_norm_w, w_r, w_s)


def _out_body(x_ref, m_ref, wo_ref, nw_ref, o_ref):
    xo = x_ref[...] + _dot(m_ref[...], wo_ref[...])
    ms = jnp.mean(xo * xo, axis=-1, keepdims=True)
    o_ref[...] = xo * lax.rsqrt(ms + EPS) * nw_ref[...]


def _out_proj(x2, merged, w_o, norm_f_w):
    t = x2.shape[0]
    tm = min(TM_OUT, t)
    assert t % tm == 0
    const = lambda i: (0, 0)
    return pl.pallas_call(
        _out_body,
        grid=(t // tm,),
        in_specs=[
            pl.BlockSpec((tm, D_MODEL), lambda i: (i, 0)),
            pl.BlockSpec((tm, D_MODEL), lambda i: (i, 0)),
            pl.BlockSpec((D_MODEL, D_MODEL), const, pipeline_mode=pl.Buffered(1)),
            pl.BlockSpec((1, D_MODEL), const),
        ],
        out_specs=pl.BlockSpec((tm, D_MODEL), lambda i: (i, 0)),
        out_shape=jax.ShapeDtypeStruct((t, D_MODEL), f32),
        compiler_params=_params(1),
        name="out_proj",
    )(x2, merged, w_o, norm_f_w)


def kernel(x, positions, norm1_w, w_in, conv_w, conv_b, dt_bias, a_log, d_skip,
           ssd_norm_w, w_br_ret, w_br_ssd, w_out, norm_f_w):
    batch, seq, d = x.shape
    assert d == D_MODEL and w_in.shape[0] == 1
    t = batch * seq
    x2 = x.reshape(t, d)
    w = jnp.swapaxes(w_in, 1, 2)

    w_gates = w[:, OFF_GATES:, :]
    pos_b = jnp.broadcast_to(positions.astype(f32).reshape(t, 1), (t, ROPE_HALF))
    inv_freq = (ROPE_THETA ** (-jnp.arange(ROPE_HALF, dtype=f32) / ROPE_HALF)).reshape(1, ROPE_HALF)

    h, dt_t, cos, sin = _prenorm(x2, norm1_w[0].reshape(1, d), w, pos_b, inv_freq)
    qk = _proj(h, w, OFF_Q, 2 * RET_WIDTH, "rope", seq, (cos, sin))
    v = _proj(h, w, OFF_V, RET_WIDTH, "none", seq)
    gz = _proj(h, w, OFF_G, RET_WIDTH + SSD_WIDTH, "silu", seq)
    xbc = _proj(h, w, OFF_XBC, SSD_CONV_DIM, "conv", seq,
                (conv_w[0].astype(f32), conv_b[0].astype(f32).reshape(1, SSD_CONV_DIM)))
    gates = _proj(h, w_gates, 0, 2 * D_MODEL, "sigmoid", seq)

    y_r = _retention(qk, v, gz, batch, seq)
    y_s = _ssd(xbc, gz, dt_t, dt_bias[0], a_log[0], d_skip[0], batch, seq)
    merged = _merge(y_r, y_s, gates, ssd_norm_w[0].reshape(1, SSD_WIDTH),
                    w_br_ret[0].astype(bf16), w_br_ssd[0].astype(bf16))
    out = _out_proj(x2, merged, w_out[0].astype(bf16), norm_f_w.reshape(1, d))
    return out.reshape(batch, seq, d)
```

```python
import functools
import math

import numpy as np
import jax
import jax.numpy as jnp
from jax import lax
from jax.experimental import pallas as pl
from jax.experimental.pallas import tpu as pltpu

f32 = jnp.float32
bf16 = jnp.bfloat16

D_MODEL = 2048
CHUNK = 64
EPS = 1e-6

RET_HEADS = 8
RET_DK = 256
RET_DV = 256
RET_WIDTH = RET_HEADS * RET_DV
ROPE_THETA = 10000.0
ROPE_HALF = RET_DK // 2

SSD_WIDTH = 2 * D_MODEL
SSD_HEADDIM = 64
SSD_HEADS = SSD_WIDTH // SSD_HEADDIM
SSD_GROUPS = 8
SSD_HPG = SSD_HEADS // SSD_GROUPS
SSD_STATE = 128
SSD_CONV = 4
SSD_GW = SSD_HPG * SSD_HEADDIM
SSD_CONV_DIM = SSD_WIDTH + 2 * SSD_GROUPS * SSD_STATE

OFF_Q = 0
OFF_K = OFF_Q + RET_WIDTH
OFF_V = OFF_K + RET_WIDTH
OFF_G = OFF_V + RET_WIDTH
OFF_Z = OFF_G + RET_WIDTH
OFF_XBC = OFF_Z + SSD_WIDTH
OFF_DT = OFF_XBC + SSD_CONV_DIM
OFF_GATES = OFF_DT + SSD_HEADS
DT_PAD = 128

LANES = 128
SUBLANES = 8
VMEM_LIMIT = 56 * 1024 * 1024
LOG2E = 1.0 / math.log(2.0)

TM_PRE = 1024
TM_IN = 1024
TN_IN = 1024
MCHUNK = 128
SLAB = 256
RET_BLK = 256
RET_HB = 8
SSD_TB = 1024
SSD_L = 128
TM_MERGE = 256
TM_OUT = 512


def _sigmoid(x):
    return 1.0 / (1.0 + jnp.exp2(x * (-LOG2E)))


def _silu(x):
    return x * _sigmoid(x)


def _dot(a, b, ca=1, cb=0):
    return lax.dot_general(a, b, (((ca,), (cb,)), ((), ())), preferred_element_type=f32)


def _params(n_axes):
    return pltpu.CompilerParams(dimension_semantics=("arbitrary",) * n_axes,
                                vmem_limit_bytes=VMEM_LIMIT)


def _prenorm_body(x_ref, nw_ref, wdt_ref, pos_ref, freq_ref, h_ref, dt_ref, cos_ref, sin_ref):
    x = x_ref[...]
    ms = jnp.mean(x * x, axis=-1, keepdims=True)
    h = (x * lax.rsqrt(ms + EPS) * nw_ref[...]).astype(bf16)
    h_ref[...] = h
    dt_ref[...] = _dot(wdt_ref[...].astype(bf16), h, 1, 1)
    ang = pos_ref[...] * freq_ref[...]
    cos_ref[...] = jnp.cos(ang)
    sin_ref[...] = jnp.sin(ang)


def _prenorm(x2, norm_w, w_dt, pos_b, inv_freq):
    t = x2.shape[0]
    tm = min(TM_PRE, t)
    assert t % tm == 0
    const = lambda i: (0, 0)
    return pl.pallas_call(
        _prenorm_body,
        grid=(t // tm,),
        in_specs=[
            pl.BlockSpec((tm, D_MODEL), lambda i: (i, 0)),
            pl.BlockSpec((1, D_MODEL), const),
            pl.BlockSpec((None, DT_PAD, D_MODEL), lambda i: (0, OFF_DT // DT_PAD, 0)),
            pl.BlockSpec((tm, ROPE_HALF), lambda i: (i, 0)),
            pl.BlockSpec((1, ROPE_HALF), const),
        ],
        out_specs=[
            pl.BlockSpec((tm, D_MODEL), lambda i: (i, 0)),
            pl.BlockSpec((DT_PAD, tm), lambda i: (0, i)),
            pl.BlockSpec((tm, ROPE_HALF), lambda i: (i, 0)),
            pl.BlockSpec((tm, ROPE_HALF), lambda i: (i, 0)),
        ],
        out_shape=[
            jax.ShapeDtypeStruct((t, D_MODEL), bf16),
            jax.ShapeDtypeStruct((DT_PAD, t), f32),
            jax.ShapeDtypeStruct((t, ROPE_HALF), f32),
            jax.ShapeDtypeStruct((t, ROPE_HALF), f32),
        ],
        compiler_params=_params(1),
        name="prenorm",
    )(x2, norm_w, w_dt, pos_b, inv_freq)


def _proj_body(*refs, epilogue, tiles_per_seq, k_first_col_block):
    if epilogue == "rope":
        h_ref, w_ref, cos_ref, sin_ref, o_ref, wb_ref = refs
    elif epilogue == "conv":
        h_ref, w_ref, cw_ref, cb_ref, o_ref, wb_ref, halo_ref = refs
    else:
        h_ref, w_ref, o_ref, wb_ref = refs
    j = pl.program_id(0)
    i = pl.program_id(1)
    tm, tn = o_ref.shape

    @pl.when(i == 0)
    def _():
        for c in range(0, D_MODEL, SLAB):
            wb_ref[c:c + SLAB, :] = w_ref[:, c:c + SLAB].T.astype(bf16)

    if epilogue == "conv":
        @pl.when(i % tiles_per_seq == 0)
        def _():
            halo_ref[...] = jnp.zeros((SUBLANES, tn), f32)

    if epilogue == "rope":
        scale = jnp.where(j >= k_first_col_block, RET_DK ** -0.5, 1.0).astype(f32)
        cos = cos_ref[...] * scale
        sin = sin_ref[...] * scale

    for n in range(tn // SLAB):
        c0 = n * SLAB
        if epilogue == "conv":
            acc = _dot(h_ref[...], wb_ref[:, c0:c0 + SLAB])
            ext = jnp.concatenate([halo_ref[:, c0:c0 + SLAB], acc], axis=0)
            w = cw_ref[:, c0:c0 + SLAB]
            x1 = pltpu.roll(ext, 1, axis=0)
            a2 = pltpu.roll(ext * w[1:2] + x1 * w[0:1], 2, axis=0)
            out = _silu((ext * w[3:4] + x1 * w[2:3] + a2 + cb_ref[:, c0:c0 + SLAB])[SUBLANES:])
            halo_ref[:, c0:c0 + SLAB] = acc[tm - SUBLANES:tm]
            o_ref[:, c0:c0 + SLAB] = out.astype(o_ref.dtype)
            continue
        for r0 in range(0, tm, MCHUNK):
            acc = _dot(h_ref[r0:r0 + MCHUNK, :], wb_ref[:, c0:c0 + SLAB])
            if epilogue == "none":
                out = acc
            elif epilogue == "silu":
                out = _silu(acc)
            elif epilogue == "sigmoid":
                out = _sigmoid(acc)
            elif epilogue == "rope":
                t1 = acc[:, :ROPE_HALF]
                t2 = acc[:, ROPE_HALF:]
                cs = cos[r0:r0 + MCHUNK]
                sn = sin[r0:r0 + MCHUNK]
                out = jnp.concatenate([t1 * cs - t2 * sn, t2 * cs + t1 * sn], axis=-1)
            o_ref[r0:r0 + MCHUNK, c0:c0 + SLAB] = out.astype(o_ref.dtype)


def _proj(h, w, col0, ncols, epilogue, seq, extra=()):
    t = h.shape[0]
    tm = min(TM_IN, t, seq)
    tn = TN_IN
    assert t % tm == 0 and seq % tm == 0 and ncols % tn == 0 and col0 % tn == 0
    jb = col0 // tn
    in_specs = [
        pl.BlockSpec((tm, D_MODEL), lambda j, i: (i, 0)),
        pl.BlockSpec((None, tn, D_MODEL), lambda j, i: (0, jb + j, 0)),
    ]
    scratch = [pltpu.VMEM((D_MODEL, tn), bf16)]
    if epilogue == "rope":
        in_specs += [pl.BlockSpec((tm, ROPE_HALF), lambda j, i: (i, 0))] * 2
    elif epilogue == "conv":
        cj = (col0 - OFF_XBC) // tn
        in_specs += [pl.BlockSpec((SSD_CONV, tn), lambda j, i: (0, cj + j)),
                     pl.BlockSpec((1, tn), lambda j, i: (0, cj + j))]
        scratch.append(pltpu.VMEM((SUBLANES, tn), f32))
    body = functools.partial(_proj_body, epilogue=epilogue, tiles_per_seq=seq // tm,
                             k_first_col_block=(OFF_K - col0) // tn)
    return pl.pallas_call(
        body,
        grid=(ncols // tn, t // tm),
        in_specs=in_specs,
        out_specs=pl.BlockSpec((tm, tn), lambda j, i: (i, j)),
        out_shape=jax.ShapeDtypeStruct((t, ncols), bf16),
        scratch_shapes=scratch,
        compiler_params=_params(2),
        name="proj_" + epilogue,
    )(h, w, *extra)


def _retention_body(q_ref, k_ref, v_ref, g_ref, dmask_ref, qdec_ref, kdec_ref, cdec_ref,
                    o_ref, state_ref):
    blk = pl.program_id(1)
    h0 = pl.program_id(2) * RET_HB

    @pl.when(blk == 0)
    def _():
        for hh in range(RET_HB):
            state_ref[h0 + hh] = jnp.zeros((RET_DK, RET_DV), f32)

    for hh in range(RET_HB):
        h = h0 + hh
        cs = slice(hh * RET_DK, (hh + 1) * RET_DK)
        qb = q_ref[:, cs]
        kb = k_ref[:, cs]
        v = v_ref[:, cs]
        kd = (kb.astype(f32) * kdec_ref[h]).astype(bf16)

        scores = _dot(qb, kb, 1, 1) * dmask_ref[h]
        state = state_ref[h]
        y = _dot(scores.astype(bf16), v)
        y = y + _dot(qb, state.astype(bf16)) * qdec_ref[h]
        state_ref[h] = state * cdec_ref[h] + _dot(kd, v, 0, 0)

        mu = jnp.mean(y, axis=-1, keepdims=True)
        yc = y - mu
        var = jnp.mean(yc * yc, axis=-1, keepdims=True)
        yn = yc * lax.rsqrt(var + EPS)
        o_ref[:, cs] = (yn * g_ref[:, cs].astype(f32)).astype(o_ref.dtype)


def _retention_tables():
    log_gamma = jnp.log1p(-(2.0 ** (-5.0 - jnp.arange(RET_HEADS, dtype=f32))))
    lg = log_gamma[:, None, None]
    n = jnp.arange(RET_BLK, dtype=f32)
    diff = n[:, None] - n[None, :]
    ci = (jnp.arange(RET_BLK) // CHUNK)
    same = ci[:, None] == ci[None, :]
    later = ci[:, None] > ci[None, :]
    expo = jnp.where(same[None], jnp.abs(diff)[None], diff[None]) * lg
    dmask = jnp.where((same | later)[None], jnp.exp(expo), 0.0).astype(f32)
    qdec = jnp.exp((n[None, :, None] + 1.0) * lg)
    kdec = jnp.exp((RET_BLK - 1.0 - n[None, :, None]) * lg)
    cdec = jnp.exp(RET_BLK * lg)
    qdec = jnp.broadcast_to(qdec, (RET_HEADS, RET_BLK, RET_DV))
    kdec = jnp.broadcast_to(kdec, (RET_HEADS, RET_BLK, RET_DK))
    cdec = jnp.broadcast_to(cdec, (RET_HEADS, RET_DK, RET_DV))
    return dmask, qdec, kdec, cdec


def _retention(qk, v, gz, batch, seq):
    t = batch * seq
    nblk = seq // RET_BLK
    assert seq % RET_BLK == 0
    dmask, qdec, kdec, cdec = _retention_tables()

    def rows(b, n, h):
        return b * nblk + n

    const3 = lambda b, n, h: (0, 0, 0)
    hgroups = RET_HEADS // RET_HB
    wide = RET_HB * RET_DK
    return pl.pallas_call(
        _retention_body,
        grid=(batch, nblk, hgroups),
        in_specs=[
            pl.BlockSpec((RET_BLK, wide), lambda b, n, h: (rows(b, n, h), h)),
            pl.BlockSpec((RET_BLK, wide), lambda b, n, h: (rows(b, n, h), hgroups + h)),
            pl.BlockSpec((RET_BLK, wide), lambda b, n, h: (rows(b, n, h), h)),
            pl.BlockSpec((RET_BLK, wide), lambda b, n, h: (rows(b, n, h), h)),
            pl.BlockSpec((RET_HEADS, RET_BLK, RET_BLK), const3),
            pl.BlockSpec((RET_HEADS, RET_BLK, RET_DV), const3),
            pl.BlockSpec((RET_HEADS, RET_BLK, RET_DK), const3),
            pl.BlockSpec((RET_HEADS, RET_DK, RET_DV), const3),
        ],
        out_specs=pl.BlockSpec((RET_BLK, wide), lambda b, n, h: (rows(b, n, h), h)),
        out_shape=jax.ShapeDtypeStruct((t, RET_WIDTH), bf16),
        scratch_shapes=[pltpu.VMEM((RET_HEADS, RET_DK, RET_DV), f32)],
        compiler_params=_params(3),
        name="retention",
    )(qk, qk, v, gz, dmask, qdec, kdec, cdec)


def _split3(a):
    hi = a.astype(bf16).astype(f32)
    r = a - hi
    mid = r.astype(bf16).astype(f32)
    lo = (r - mid).astype(bf16).astype(f32)
    return [hi, mid, lo]


SSD_KGRP = 32
SSD_K_ACUM, SSD_K_ONES, SSD_K_EEXP, SSD_K_TAIL = 0, 32, 64, 96


def _group32(parts):
    return jnp.concatenate(list(parts) + [jnp.zeros_like(parts[0])], axis=0)


def _ssd_decay_prep(dt_raw, bias, neg_a, upper, blockdiag):
    H, L = dt_raw.shape
    raw = dt_raw + bias
    dt = jnp.maximum(raw, 0.0) + jnp.log1p(jnp.exp(-jnp.abs(raw)))
    csum = _dot(jnp.concatenate(_split3(dt * neg_a), axis=0).astype(bf16), upper)
    acum = (csum[0:H] + csum[H:2 * H]) + csum[2 * H:3 * H]
    alast = acum[:, L - 1:L]
    tailw = jnp.exp(alast - acum) * dt
    acum2 = acum * LOG2E
    srow = acum2 - jnp.maximum(jnp.log(dt) * LOG2E, -200.0)
    ones = jnp.ones_like(acum2)
    stack = jnp.concatenate([_group32(_split3(acum2)), _group32([ones, ones, ones]),
                             _group32(_split3(jnp.exp2(acum2))), _group32(_split3(tailw))], axis=0)
    lhs = stack.T.astype(bf16)
    dyn = jnp.tile(_group32([-p for p in _split3(srow)]), (1, H))
    dyn = jnp.where(blockdiag, dyn, 0.0).astype(bf16)
    return lhs, dyn


def _ssd_body(xs_ref, b_ref, c_ref, z_ref, dt_ref, dtn_ref, bias_ref, alog_ref, dexp_ref,
              rconst_ref, rexp_ref, o_ref, state_ref, lhs_ref, rhs_ref):
    blk = pl.program_id(2)
    tb = xs_ref.shape[0]
    L = SSD_L
    nsb = tb // L
    H = SSD_HPG
    GW = SSD_GW

    neg_a = -jnp.exp(alog_ref[...])
    bias = bias_ref[...]
    row = lax.broadcasted_iota(jnp.int32, (L, L), 0)
    col = lax.broadcasted_iota(jnp.int32, (L, L), 1)
    causal = row >= col
    upper = jnp.where(row <= col, 1.0, 0.0).astype(bf16)
    lane_lo = lax.broadcasted_iota(jnp.int32, (L, LANES), 1) < SSD_HEADDIM
    lane_lo2 = jnp.concatenate([lane_lo, lane_lo], axis=0)
    row_top = lax.broadcasted_iota(jnp.int32, (2 * L, LANES), 0) < L
    keep_bd = row_top == lane_lo2
    krow = lax.broadcasted_iota(jnp.int32, (SSD_KGRP, H * L), 0)
    kcol = lax.broadcasted_iota(jnp.int32, (SSD_KGRP, H * L), 1)
    blockdiag = (krow % H) == (kcol // L)

    def prep(dt_raw):
        return _ssd_decay_prep(dt_raw, bias, neg_a, upper, blockdiag)

    @pl.when(blk == 0)
    def _():
        state_ref[...] = jnp.zeros(state_ref.shape, f32)
        for sb in range(nsb):
            lhs0, dyn0 = prep(dt_ref[:, sb * L:(sb + 1) * L])
            lhs_ref[sb] = lhs0
            rhs_ref[sb] = rconst_ref[...]
            rhs_ref[sb, SSD_K_ONES:SSD_K_ONES + SSD_KGRP, :] = dyn0

    for sb in range(nsb):
        t0 = sb * L
        lhs = lhs_ref[sb]
        segall = _dot(lhs, rhs_ref[sb])
        expall = _dot(lhs, rexp_ref[...])
        lhs_next, dyn_next = prep(dtn_ref[:, t0:t0 + L])
        xcb = xs_ref[t0:t0 + L, :]
        bcb = b_ref[t0:t0 + L, :]
        ccb = c_ref[t0:t0 + L, :]
        xc = xcb.astype(f32)

        cb = _dot(ccb, bcb, 1, 1)

        ypieces = []
        for j in range(SSD_HPG // 2):
            gs = []
            for hh in (2 * j, 2 * j + 1):
                seg = segall[:, hh * L:(hh + 1) * L]
                gs.append((cb * jnp.exp2(jnp.where(causal, seg, -jnp.inf))).astype(bf16))
            gpair = jnp.concatenate(gs, axis=1)
            xpair = xcb[:, j * LANES:(j + 1) * LANES]
            x2 = jnp.concatenate([xpair, xpair], axis=0)
            xbd = jnp.where(keep_bd, x2, jnp.zeros_like(x2))
            ypieces.append(_dot(gpair, xbd))
        y = jnp.concatenate(ypieces, axis=-1)

        state = state_ref[...]
        ycross = _dot(ccb, state.astype(bf16))
        y = y + ycross * expall[:, :GW]
        y = y + dexp_ref[...] * xc

        xw = (xc * expall[:, GW:]).astype(bf16)
        sdec = expall[L - 1:L, :GW]
        state_ref[...] = state * sdec + _dot(bcb, xw, 0, 0)

        o_ref[t0:t0 + L, :] = (y * z_ref[t0:t0 + L, :].astype(f32)).astype(o_ref.dtype)
        lhs_ref[sb] = lhs_next
        rhs_ref[sb, SSD_K_ONES:SSD_K_ONES + SSD_KGRP, :] = dyn_next


def _ssd(xbc, gz, dt_t, dt_bias, a_log, d_skip, batch, seq):
    t = batch * seq
    tb = min(SSD_TB, seq)
    nblk = seq // tb
    assert seq % tb == 0 and tb % SSD_L == 0
    L = SSD_L
    bias_b = jnp.broadcast_to(dt_bias.astype(f32)[:, None], (SSD_HEADS, L))
    alog_b = jnp.broadcast_to(a_log.astype(f32)[:, None], (SSD_HEADS, L))
    d_exp = jnp.repeat(d_skip.astype(f32), SSD_HEADDIM).reshape(1, SSD_WIDTH)

    kk = np.arange(4 * SSD_KGRP)
    k_head = kk % SSD_HPG
    k_live = (kk % SSD_KGRP) < 3 * SSD_HPG
    k_grp = kk // SSD_KGRP
    seg_lane_head = np.arange(SSD_HPG * L) // L
    rconst = ((k_grp == SSD_K_ACUM // SSD_KGRP) & k_live)[:, None] & (k_head[:, None] == seg_lane_head[None, :])
    x_lane_head = np.arange(SSD_GW) // SSD_HEADDIM
    hit = k_live[:, None] & (k_head[:, None] == x_lane_head[None, :])
    rexp = np.concatenate([hit & (k_grp == SSD_K_EEXP // SSD_KGRP)[:, None],
                           hit & (k_grp == SSD_K_TAIL // SSD_KGRP)[:, None]], axis=1)
    rconst = jnp.asarray(rconst, dtype=bf16)
    rexp = jnp.asarray(rexp, dtype=bf16)

    cz = RET_WIDTH // SSD_GW
    cbm = SSD_WIDTH // SSD_STATE
    ccm = cbm + SSD_GROUPS

    def rows(b, g, n):
        return b * nblk + n

    def next_rows(b, g, n):
        return jnp.minimum(b * nblk + n + 1, batch * nblk - 1)

    return pl.pallas_call(
        _ssd_body,
        grid=(batch, SSD_GROUPS, nblk),
        in_specs=[
            pl.BlockSpec((tb, SSD_GW), lambda b, g, n: (rows(b, g, n), g)),
            pl.BlockSpec((tb, SSD_STATE), lambda b, g, n: (rows(b, g, n), cbm + g)),
            pl.BlockSpec((tb, SSD_STATE), lambda b, g, n: (rows(b, g, n), ccm + g)),
            pl.BlockSpec((tb, SSD_GW), lambda b, g, n: (rows(b, g, n), cz + g)),
            pl.BlockSpec((SSD_HPG, tb), lambda b, g, n: (g, rows(b, g, n))),
            pl.BlockSpec((SSD_HPG, tb), lambda b, g, n: (g, next_rows(b, g, n))),
            pl.BlockSpec((SSD_HPG, L), lambda b, g, n: (g, 0)),
            pl.BlockSpec((SSD_HPG, L), lambda b, g, n: (g, 0)),
            pl.BlockSpec((1, SSD_GW), lambda b, g, n: (0, g)),
            pl.BlockSpec((4 * SSD_KGRP, SSD_HPG * L), lambda b, g, n: (0, 0)),
            pl.BlockSpec((4 * SSD_KGRP, 2 * SSD_GW), lambda b, g, n: (0, 0)),
        ],
        out_specs=pl.BlockSpec((tb, SSD_GW), lambda b, g, n: (rows(b, g, n), g)),
        out_shape=jax.ShapeDtypeStruct((t, SSD_WIDTH), bf16),
        scratch_shapes=[pltpu.VMEM((SSD_STATE, SSD_GW), f32),
                        pltpu.VMEM((tb // L, L, 4 * SSD_KGRP), bf16),
                        pltpu.VMEM((tb // L, 4 * SSD_KGRP, SSD_HPG * L), bf16)],
        compiler_params=_params(3),
        name="ssd",
    )(xbc, xbc, xbc, gz, dt_t, dt_t, bias_b, alog_b, d_exp, rconst, rexp)


def _merge_body(yr_ref, ys_ref, gr_ref, gs_ref, wr_ref, ws_ref, o_ref):
    ys = ys_ref[...]
    ysf = ys.astype(f32)
    rinv = lax.rsqrt(jnp.mean(ysf * ysf, axis=-1, keepdims=True) + EPS)
    yr = yr_ref[...]
    for c0 in range(0, D_MODEL, SLAB):
        p_r = _dot(yr, wr_ref[:, c0:c0 + SLAB])
        p_s = _dot(ys, ws_ref[:, c0:c0 + SLAB]) * rinv
        merged = (gr_ref[:, c0:c0 + SLAB].astype(f32) * p_r
                  + gs_ref[:, c0:c0 + SLAB].astype(f32) * p_s)
        o_ref[:, c0:c0 + SLAB] = merged.astype(o_ref.dtype)


def _merge(y_r, y_s, gates, w_r, w_s):
    t = y_r.shape[0]
    tm = min(TM_MERGE, t)
    assert t % tm == 0
    const = lambda i: (0, 0)
    return pl.pallas_call(
        _merge_body,
        grid=(t // tm,),
        in_specs=[
            pl.BlockSpec((tm, RET_WIDTH), lambda i: (i, 0)),
            pl.BlockSpec((tm, SSD_WIDTH), lambda i: (i, 0)),
            pl.BlockSpec((tm, D_MODEL), lambda i: (i, 0)),
            pl.BlockSpec((tm, D_MODEL), lambda i: (i, 1)),
            pl.BlockSpec((RET_WIDTH, D_MODEL), const, pipeline_mode=pl.Buffered(1)),
            pl.BlockSpec((SSD_WIDTH, D_MODEL), const, pipeline_mode=pl.Buffered(1)),
        ],
        out_specs=pl.BlockSpec((tm, D_MODEL), lambda i: (i, 0)),
        out_shape=jax.ShapeDtypeStruct((t, D_MODEL), bf16),
        compiler_params=_params(1),
        name="merge",
    )(y_r, y_s, gates, gates, w_r, w_s)


def _out_body(x_ref, m_ref, wo_ref, nw_ref, o_ref):
    xo = x_ref[...] + _dot(m_ref[...], wo_ref[...])
    ms = jnp.mean(xo * xo, axis=-1, keepdims=True)
    o_ref[...] = xo * lax.rsqrt(ms + EPS) * nw_ref[...]


def _out_proj(x2, merged, w_o, norm_f_w):
    t = x2.shape[0]
    tm = min(TM_OUT, t)
    assert t % tm == 0
    const = lambda i: (0, 0)
    return pl.pallas_call(
        _out_body,
        grid=(t // tm,),
        in_specs=[
            pl.BlockSpec((tm, D_MODEL), lambda i: (i, 0)),
            pl.BlockSpec((tm, D_MODEL), lambda i: (i, 0)),
            pl.BlockSpec((D_MODEL, D_MODEL), const, pipeline_mode=pl.Buffered(1)),
            pl.BlockSpec((1, D_MODEL), const),
        ],
        out_specs=pl.BlockSpec((tm, D_MODEL), lambda i: (i, 0)),
        out_shape=jax.ShapeDtypeStruct((t, D_MODEL), f32),
        compiler_params=_params(1),
        name="out_proj",
    )(x2, merged, w_o, norm_f_w)


def kernel(x, positions, norm1_w, w_in, conv_w, conv_b, dt_bias, a_log, d_skip,
           ssd_norm_w, w_br_ret, w_br_ssd, w_out, norm_f_w):
    batch, seq, d = x.shape
    assert d == D_MODEL and w_in.shape[0] == 1
    t = batch * seq
    x2 = x.reshape(t, d)
    w = jnp.swapaxes(w_in, 1, 2)

    w_gates = w[:, OFF_GATES:, :]
    pos_b = jnp.broadcast_to(positions.astype(f32).reshape(t, 1), (t, ROPE_HALF))
    inv_freq = (ROPE_THETA ** (-jnp.arange(ROPE_HALF, dtype=f32) / ROPE_HALF)).reshape(1, ROPE_HALF)

    h, dt_t, cos, sin = _prenorm(x2, norm1_w[0].reshape(1, d), w, pos_b, inv_freq)
    qk = _proj(h, w, OFF_Q, 2 * RET_WIDTH, "rope", seq, (cos, sin))
    v = _proj(h, w, OFF_V, RET_WIDTH, "none", seq)
    gz = _proj(h, w, OFF_G, RET_WIDTH + SSD_WIDTH, "silu", seq)
    xbc = _proj(h, w, OFF_XBC, SSD_CONV_DIM, "conv", seq,
                (conv_w[0].astype(f32), conv_b[0].astype(f32).reshape(1, SSD_CONV_DIM)))
    gates = _proj(h, w_gates, 0, 2 * D_MODEL, "sigmoid", seq)

    y_r = _retention(qk, v, gz, batch, seq)
    y_s = _ssd(xbc, gz, dt_t, dt_bias[0], a_log[0], d_skip[0], batch, seq)
    w_s = (ssd_norm_w[0].astype(f32)[:, None] * w_br_ssd[0]).astype(bf16)
    merged = _merge(y_r, y_s, gates, w_br_ret[0].astype(bf16), w_s)
    out = _out_proj(x2, merged, w_out[0].astype(bf16), norm_f_w.reshape(1, d))
    return out.reshape(batch, seq, d)
```

```python
import functools
import math

import numpy as np
import jax
import jax.numpy as jnp
from jax import lax
from jax.experimental import pallas as pl
from jax.experimental.pallas import tpu as pltpu

f32 = jnp.float32
bf16 = jnp.bfloat16

D_MODEL = 2048
CHUNK = 64
EPS = 1e-6

RET_HEADS = 8
RET_DK = 256
RET_DV = 256
RET_WIDTH = RET_HEADS * RET_DV
ROPE_THETA = 10000.0
ROPE_HALF = RET_DK // 2

SSD_WIDTH = 2 * D_MODEL
SSD_HEADDIM = 64
SSD_HEADS = SSD_WIDTH // SSD_HEADDIM
SSD_GROUPS = 8
SSD_HPG = SSD_HEADS // SSD_GROUPS
SSD_STATE = 128
SSD_CONV = 4
SSD_GW = SSD_HPG * SSD_HEADDIM
SSD_CONV_DIM = SSD_WIDTH + 2 * SSD_GROUPS * SSD_STATE

OFF_Q = 0
OFF_K = OFF_Q + RET_WIDTH
OFF_V = OFF_K + RET_WIDTH
OFF_G = OFF_V + RET_WIDTH
OFF_Z = OFF_G + RET_WIDTH
OFF_XBC = OFF_Z + SSD_WIDTH
OFF_DT = OFF_XBC + SSD_CONV_DIM
OFF_GATES = OFF_DT + SSD_HEADS
DT_PAD = 128

LANES = 128
SUBLANES = 8
VMEM_LIMIT = 56 * 1024 * 1024
LOG2E = 1.0 / math.log(2.0)

TM_PRE = 1024
TM_IN = 1024
TN_IN = 1024
MCHUNK = 128
SLAB = 256
RET_BLK = 256
RET_HB = 8
SSD_TB = 1024
SSD_L = 128
TM_MERGE = 256
TM_OUT = 512


def _sigmoid(x):
    return 1.0 / (1.0 + jnp.exp2(x * (-LOG2E)))


def _silu(x):
    return x * _sigmoid(x)


def _dot(a, b, ca=1, cb=0):
    return lax.dot_general(a, b, (((ca,), (cb,)), ((), ())), preferred_element_type=f32)


def _params(n_axes):
    return pltpu.CompilerParams(dimension_semantics=("arbitrary",) * n_axes,
                                vmem_limit_bytes=VMEM_LIMIT)


def _prenorm_body(x_ref, nw_ref, wdt_ref, pos_ref, freq_ref, h_ref, dt_ref, cos_ref, sin_ref):
    x = x_ref[...]
    ms = jnp.mean(x * x, axis=-1, keepdims=True)
    h = (x * lax.rsqrt(ms + EPS) * nw_ref[...]).astype(bf16)
    h_ref[...] = h
    dt_ref[...] = _dot(wdt_ref[...].astype(bf16), h, 1, 1)
    ang = pos_ref[...] * freq_ref[...]
    cos_ref[...] = jnp.cos(ang)
    sin_ref[...] = jnp.sin(ang)


def _prenorm(x2, norm_w, w_dt, pos_b, inv_freq):
    t = x2.shape[0]
    tm = min(TM_PRE, t)
    assert t % tm == 0
    const = lambda i: (0, 0)
    return pl.pallas_call(
        _prenorm_body,
        grid=(t // tm,),
        in_specs=[
            pl.BlockSpec((tm, D_MODEL), lambda i: (i, 0)),
            pl.BlockSpec((1, D_MODEL), const),
            pl.BlockSpec((None, DT_PAD, D_MODEL), lambda i: (0, OFF_DT // DT_PAD, 0)),
            pl.BlockSpec((tm, ROPE_HALF), lambda i: (i, 0)),
            pl.BlockSpec((1, ROPE_HALF), const),
        ],
        out_specs=[
            pl.BlockSpec((tm, D_MODEL), lambda i: (i, 0)),
            pl.BlockSpec((DT_PAD, tm), lambda i: (0, i)),
            pl.BlockSpec((tm, ROPE_HALF), lambda i: (i, 0)),
            pl.BlockSpec((tm, ROPE_HALF), lambda i: (i, 0)),
        ],
        out_shape=[
            jax.ShapeDtypeStruct((t, D_MODEL), bf16),
            jax.ShapeDtypeStruct((DT_PAD, t), f32),
            jax.ShapeDtypeStruct((t, ROPE_HALF), f32),
            jax.ShapeDtypeStruct((t, ROPE_HALF), f32),
        ],
        compiler_params=_params(1),
        name="prenorm",
    )(x2, norm_w, w_dt, pos_b, inv_freq)


def _proj_body(*refs, epilogue, tiles_per_seq, k_first_col_block):
    if epilogue == "rope":
        h_ref, w_ref, cos_ref, sin_ref, o_ref, wb_ref = refs
    elif epilogue == "conv":
        h_ref, w_ref, cw_ref, cb_ref, o_ref, wb_ref, halo_ref = refs
    else:
        h_ref, w_ref, o_ref, wb_ref = refs
    j = pl.program_id(0)
    i = pl.program_id(1)
    tm, tn = o_ref.shape

    @pl.when(i == 0)
    def _():
        for c in range(0, D_MODEL, SLAB):
            wb_ref[c:c + SLAB, :] = w_ref[:, c:c + SLAB].T.astype(bf16)

    if epilogue == "conv":
        @pl.when(i % tiles_per_seq == 0)
        def _():
            halo_ref[...] = jnp.zeros((SUBLANES, tn), f32)

    if epilogue == "rope":
        scale = jnp.where(j >= k_first_col_block, RET_DK ** -0.5, 1.0).astype(f32)
        cos = cos_ref[...] * scale
        sin = sin_ref[...] * scale

    for n in range(tn // SLAB):
        c0 = n * SLAB
        if epilogue == "conv":
            acc = _dot(h_ref[...], wb_ref[:, c0:c0 + SLAB])
            ext = jnp.concatenate([halo_ref[:, c0:c0 + SLAB], acc], axis=0)
            w = cw_ref[:, c0:c0 + SLAB]
            x1 = pltpu.roll(ext, 1, axis=0)
            a2 = pltpu.roll(ext * w[1:2] + x1 * w[0:1], 2, axis=0)
            out = (ext * w[3:4] + x1 * w[2:3] + a2 + cb_ref[:, c0:c0 + SLAB])[SUBLANES:]
            halo_ref[:, c0:c0 + SLAB] = acc[tm - SUBLANES:tm]
            o_ref[:, c0:c0 + SLAB] = out.astype(o_ref.dtype)
            continue
        for r0 in range(0, tm, MCHUNK):
            acc = _dot(h_ref[r0:r0 + MCHUNK, :], wb_ref[:, c0:c0 + SLAB])
            if epilogue == "none":
                out = acc
            elif epilogue == "silu":
                out = _silu(acc)
            elif epilogue == "sigmoid":
                out = _sigmoid(acc)
            elif epilogue == "rope":
                t1 = acc[:, :ROPE_HALF]
                t2 = acc[:, ROPE_HALF:]
                cs = cos[r0:r0 + MCHUNK]
                sn = sin[r0:r0 + MCHUNK]
                out = jnp.concatenate([t1 * cs - t2 * sn, t2 * cs + t1 * sn], axis=-1)
            o_ref[r0:r0 + MCHUNK, c0:c0 + SLAB] = out.astype(o_ref.dtype)


def _proj(h, w, col0, ncols, epilogue, seq, extra=()):
    t = h.shape[0]
    tm = min(TM_IN, t, seq)
    tn = TN_IN
    assert t % tm == 0 and seq % tm == 0 and ncols % tn == 0 and col0 % tn == 0
    jb = col0 // tn
    in_specs = [
        pl.BlockSpec((tm, D_MODEL), lambda j, i: (i, 0)),
        pl.BlockSpec((None, tn, D_MODEL), lambda j, i: (0, jb + j, 0)),
    ]
    scratch = [pltpu.VMEM((D_MODEL, tn), bf16)]
    if epilogue == "rope":
        in_specs += [pl.BlockSpec((tm, ROPE_HALF), lambda j, i: (i, 0))] * 2
    elif epilogue == "conv":
        cj = (col0 - OFF_XBC) // tn
        in_specs += [pl.BlockSpec((SSD_CONV, tn), lambda j, i: (0, cj + j)),
                     pl.BlockSpec((1, tn), lambda j, i: (0, cj + j))]
        scratch.append(pltpu.VMEM((SUBLANES, tn), f32))
    body = functools.partial(_proj_body, epilogue=epilogue, tiles_per_seq=seq // tm,
                             k_first_col_block=(OFF_K - col0) // tn)
    return pl.pallas_call(
        body,
        grid=(ncols // tn, t // tm),
        in_specs=in_specs,
        out_specs=pl.BlockSpec((tm, tn), lambda j, i: (i, j)),
        out_shape=jax.ShapeDtypeStruct((t, ncols), bf16),
        scratch_shapes=scratch,
        compiler_params=_params(2),
        name="proj_" + epilogue,
    )(h, w, *extra)


def _retention_body(q_ref, k_ref, v_ref, g_ref, dmask_ref, qdec_ref, kdec_ref, cdec_ref,
                    o_ref, state_ref):
    blk = pl.program_id(1)
    h0 = pl.program_id(2) * RET_HB

    @pl.when(blk == 0)
    def _():
        for hh in range(RET_HB):
            state_ref[h0 + hh] = jnp.zeros((RET_DK, RET_DV), f32)

    for hh in range(RET_HB):
        h = h0 + hh
        cs = slice(hh * RET_DK, (hh + 1) * RET_DK)
        qb = q_ref[:, cs]
        kb = k_ref[:, cs]
        v = v_ref[:, cs]
        kd = (kb.astype(f32) * kdec_ref[h]).astype(bf16)

        scores = _dot(qb, kb, 1, 1) * dmask_ref[h]
        state = state_ref[h]
        y = _dot(scores.astype(bf16), v)
        y = y + _dot(qb, state.astype(bf16)) * qdec_ref[h]
        state_ref[h] = state * cdec_ref[h] + _dot(kd, v, 0, 0)

        mu = jnp.mean(y, axis=-1, keepdims=True)
        yc = y - mu
        var = jnp.mean(yc * yc, axis=-1, keepdims=True)
        yn = yc * lax.rsqrt(var + EPS)
        o_ref[:, cs] = (yn * g_ref[:, cs].astype(f32)).astype(o_ref.dtype)


def _retention_tables():
    log_gamma = jnp.log1p(-(2.0 ** (-5.0 - jnp.arange(RET_HEADS, dtype=f32))))
    lg = log_gamma[:, None, None]
    n = jnp.arange(RET_BLK, dtype=f32)
    diff = n[:, None] - n[None, :]
    ci = (jnp.arange(RET_BLK) // CHUNK)
    same = ci[:, None] == ci[None, :]
    later = ci[:, None] > ci[None, :]
    expo = jnp.where(same[None], jnp.abs(diff)[None], diff[None]) * lg
    dmask = jnp.where((same | later)[None], jnp.exp(expo), 0.0).astype(f32)
    qdec = jnp.exp((n[None, :, None] + 1.0) * lg)
    kdec = jnp.exp((RET_BLK - 1.0 - n[None, :, None]) * lg)
    cdec = jnp.exp(RET_BLK * lg)
    qdec = jnp.broadcast_to(qdec, (RET_HEADS, RET_BLK, RET_DV))
    kdec = jnp.broadcast_to(kdec, (RET_HEADS, RET_BLK, RET_DK))
    cdec = jnp.broadcast_to(cdec, (RET_HEADS, RET_DK, RET_DV))
    return dmask, qdec, kdec, cdec


def _retention(qk, v, gz, batch, seq):
    t = batch * seq
    nblk = seq // RET_BLK
    assert seq % RET_BLK == 0
    dmask, qdec, kdec, cdec = _retention_tables()

    def rows(b, n, h):
        return b * nblk + n

    const3 = lambda b, n, h: (0, 0, 0)
    hgroups = RET_HEADS // RET_HB
    wide = RET_HB * RET_DK
    return pl.pallas_call(
        _retention_body,
        grid=(batch, nblk, hgroups),
        in_specs=[
            pl.BlockSpec((RET_BLK, wide), lambda b, n, h: (rows(b, n, h), h)),
            pl.BlockSpec((RET_BLK, wide), lambda b, n, h: (rows(b, n, h), hgroups + h)),
            pl.BlockSpec((RET_BLK, wide), lambda b, n, h: (rows(b, n, h), h)),
            pl.BlockSpec((RET_BLK, wide), lambda b, n, h: (rows(b, n, h), h)),
            pl.BlockSpec((RET_HEADS, RET_BLK, RET_BLK), const3),
            pl.BlockSpec((RET_HEADS, RET_BLK, RET_DV), const3),
            pl.BlockSpec((RET_HEADS, RET_BLK, RET_DK), const3),
            pl.BlockSpec((RET_HEADS, RET_DK, RET_DV), const3),
        ],
        out_specs=pl.BlockSpec((RET_BLK, wide), lambda b, n, h: (rows(b, n, h), h)),
        out_shape=jax.ShapeDtypeStruct((t, RET_WIDTH), bf16),
        scratch_shapes=[pltpu.VMEM((RET_HEADS, RET_DK, RET_DV), f32)],
        compiler_params=_params(3),
        name="retention",
    )(qk, qk, v, gz, dmask, qdec, kdec, cdec)


def _split3(a):
    hi = a.astype(bf16).astype(f32)
    r = a - hi
    mid = r.astype(bf16).astype(f32)
    lo = (r - mid).astype(bf16).astype(f32)
    return [hi, mid, lo]


SSD_KGRP = 32
SSD_K_ACUM, SSD_K_ONES, SSD_K_EEXP, SSD_K_TAIL = 0, 32, 64, 96


def _group32(parts):
    return jnp.concatenate(list(parts) + [jnp.zeros_like(parts[0])], axis=0)


def _ssd_decay_prep(dt_raw, bias, neg_a, upper, blockdiag):
    H, L = dt_raw.shape
    raw = dt_raw + bias
    dt = jnp.maximum(raw, 0.0) + jnp.log1p(jnp.exp(-jnp.abs(raw)))
    csum = _dot(jnp.concatenate(_split3(dt * neg_a), axis=0).astype(bf16), upper)
    acum = (csum[0:H] + csum[H:2 * H]) + csum[2 * H:3 * H]
    alast = acum[:, L - 1:L]
    tailw = jnp.exp(alast - acum) * dt
    acum2 = acum * LOG2E
    srow = acum2 - jnp.maximum(jnp.log(dt) * LOG2E, -200.0)
    ones = jnp.ones_like(acum2)
    stack = jnp.concatenate([_group32(_split3(acum2)), _group32([ones, ones, ones]),
                             _group32(_split3(jnp.exp2(acum2))), _group32(_split3(tailw))], axis=0)
    lhs = stack.T.astype(bf16)
    dyn = jnp.tile(_group32([-p for p in _split3(srow)]), (1, H))
    dyn = jnp.where(blockdiag, dyn, 0.0).astype(bf16)
    return lhs, dyn


def _ssd_body(xs_ref, b_ref, c_ref, z_ref, dt_ref, dtn_ref, bias_ref, alog_ref, dexp_ref,
              rconst_ref, rexp_ref, o_ref, state_ref, lhs_ref, rhs_ref):
    blk = pl.program_id(2)
    tb = xs_ref.shape[0]
    L = SSD_L
    nsb = tb // L
    H = SSD_HPG
    GW = SSD_GW

    neg_a = -jnp.exp(alog_ref[...])
    bias = bias_ref[...]
    row = lax.broadcasted_iota(jnp.int32, (L, L), 0)
    col = lax.broadcasted_iota(jnp.int32, (L, L), 1)
    causal = row >= col
    upper = jnp.where(row <= col, 1.0, 0.0).astype(bf16)
    lane_lo = lax.broadcasted_iota(jnp.int32, (L, LANES), 1) < SSD_HEADDIM
    lane_lo2 = jnp.concatenate([lane_lo, lane_lo], axis=0)
    row_top = lax.broadcasted_iota(jnp.int32, (2 * L, LANES), 0) < L
    keep_bd = row_top == lane_lo2
    krow = lax.broadcasted_iota(jnp.int32, (SSD_KGRP, H * L), 0)
    kcol = lax.broadcasted_iota(jnp.int32, (SSD_KGRP, H * L), 1)
    blockdiag = (krow % H) == (kcol // L)

    def prep(dt_raw):
        return _ssd_decay_prep(dt_raw, bias, neg_a, upper, blockdiag)

    @pl.when(blk == 0)
    def _():
        state_ref[...] = jnp.zeros(state_ref.shape, f32)
        for sb in range(nsb):
            lhs0, dyn0 = prep(dt_ref[:, sb * L:(sb + 1) * L])
            lhs_ref[sb] = lhs0
            rhs_ref[sb] = rconst_ref[...]
            rhs_ref[sb, SSD_K_ONES:SSD_K_ONES + SSD_KGRP, :] = dyn0

    for sb in range(nsb):
        t0 = sb * L
        lhs = lhs_ref[sb]
        segall = _dot(lhs, rhs_ref[sb])
        expall = _dot(lhs, rexp_ref[...])
        lhs_next, dyn_next = prep(dtn_ref[:, t0:t0 + L])
        xc = _silu(xs_ref[t0:t0 + L, :].astype(f32))
        xcb = xc.astype(bf16)
        bcb = _silu(b_ref[t0:t0 + L, :].astype(f32)).astype(bf16)
        ccb = _silu(c_ref[t0:t0 + L, :].astype(f32)).astype(bf16)

        cb = _dot(ccb, bcb, 1, 1)

        ypieces = []
        for j in range(SSD_HPG // 2):
            gs = []
            for hh in (2 * j, 2 * j + 1):
                seg = segall[:, hh * L:(hh + 1) * L]
                gs.append((cb * jnp.exp2(jnp.where(causal, seg, -jnp.inf))).astype(bf16))
            gpair = jnp.concatenate(gs, axis=1)
            xpair = xcb[:, j * LANES:(j + 1) * LANES]
            x2 = jnp.concatenate([xpair, xpair], axis=0)
            xbd = jnp.where(keep_bd, x2, jnp.zeros_like(x2))
            ypieces.append(_dot(gpair, xbd))
        y = jnp.concatenate(ypieces, axis=-1)

        state = state_ref[...]
        ycross = _dot(ccb, state.astype(bf16))
        y = y + ycross * expall[:, :GW]
        y = y + dexp_ref[...] * xc

        xw = (xc * expall[:, GW:]).astype(bf16)
        sdec = expall[L - 1:L, :GW]
        state_ref[...] = state * sdec + _dot(bcb, xw, 0, 0)

        o_ref[t0:t0 + L, :] = (y * z_ref[t0:t0 + L, :].astype(f32)).astype(o_ref.dtype)
        lhs_ref[sb] = lhs_next
        rhs_ref[sb, SSD_K_ONES:SSD_K_ONES + SSD_KGRP, :] = dyn_next


def _ssd(xbc, gz, dt_t, dt_bias, a_log, d_skip, batch, seq):
    t = batch * seq
    tb = min(SSD_TB, seq)
    nblk = seq // tb
    assert seq % tb == 0 and tb % SSD_L == 0
    L = SSD_L
    bias_b = jnp.broadcast_to(dt_bias.astype(f32)[:, None], (SSD_HEADS, L))
    alog_b = jnp.broadcast_to(a_log.astype(f32)[:, None], (SSD_HEADS, L))
    d_exp = jnp.repeat(d_skip.astype(f32), SSD_HEADDIM).reshape(1, SSD_WIDTH)

    kk = np.arange(4 * SSD_KGRP)
    k_head = kk % SSD_HPG
    k_live = (kk % SSD_KGRP) < 3 * SSD_HPG
    k_grp = kk // SSD_KGRP
    seg_lane_head = np.arange(SSD_HPG * L) // L
    rconst = ((k_grp == SSD_K_ACUM // SSD_KGRP) & k_live)[:, None] & (k_head[:, None] == seg_lane_head[None, :])
    x_lane_head = np.arange(SSD_GW) // SSD_HEADDIM
    hit = k_live[:, None] & (k_head[:, None] == x_lane_head[None, :])
    rexp = np.concatenate([hit & (k_grp == SSD_K_EEXP // SSD_KGRP)[:, None],
                           hit & (k_grp == SSD_K_TAIL // SSD_KGRP)[:, None]], axis=1)
    rconst = jnp.asarray(rconst, dtype=bf16)
    rexp = jnp.asarray(rexp, dtype=bf16)

    cz = RET_WIDTH // SSD_GW
    cbm = SSD_WIDTH // SSD_STATE
    ccm = cbm + SSD_GROUPS

    def rows(b, g, n):
        return b * nblk + n

    def next_rows(b, g, n):
        return jnp.minimum(b * nblk + n + 1, batch * nblk - 1)

    return pl.pallas_call(
        _ssd_body,
        grid=(batch, SSD_GROUPS, nblk),
        in_specs=[
            pl.BlockSpec((tb, SSD_GW), lambda b, g, n: (rows(b, g, n), g)),
            pl.BlockSpec((tb, SSD_STATE), lambda b, g, n: (rows(b, g, n), cbm + g)),
            pl.BlockSpec((tb, SSD_STATE), lambda b, g, n: (rows(b, g, n), ccm + g)),
            pl.BlockSpec((tb, SSD_GW), lambda b, g, n: (rows(b, g, n), cz + g)),
            pl.BlockSpec((SSD_HPG, tb), lambda b, g, n: (g, rows(b, g, n))),
            pl.BlockSpec((SSD_HPG, tb), lambda b, g, n: (g, next_rows(b, g, n))),
            pl.BlockSpec((SSD_HPG, L), lambda b, g, n: (g, 0)),
            pl.BlockSpec((SSD_HPG, L), lambda b, g, n: (g, 0)),
            pl.BlockSpec((1, SSD_GW), lambda b, g, n: (0, g)),
            pl.BlockSpec((4 * SSD_KGRP, SSD_HPG * L), lambda b, g, n: (0, 0)),
            pl.BlockSpec((4 * SSD_KGRP, 2 * SSD_GW), lambda b, g, n: (0, 0)),
        ],
        out_specs=pl.BlockSpec((tb, SSD_GW), lambda b, g, n: (rows(b, g, n), g)),
        out_shape=jax.ShapeDtypeStruct((t, SSD_WIDTH), bf16),
        scratch_shapes=[pltpu.VMEM((SSD_STATE, SSD_GW), f32),
                        pltpu.VMEM((tb // L, L, 4 * SSD_KGRP), bf16),
                        pltpu.VMEM((tb // L, 4 * SSD_KGRP, SSD_HPG * L), bf16)],
        compiler_params=_params(3),
        name="ssd",
    )(xbc, xbc, xbc, gz, dt_t, dt_t, bias_b, alog_b, d_exp, rconst, rexp)


def _merge_body(yr_ref, ys_ref, gr_ref, gs_ref, wr_ref, ws_ref, o_ref):
    ys = ys_ref[...]
    ysf = ys.astype(f32)
    rinv = lax.rsqrt(jnp.mean(ysf * ysf, axis=-1, keepdims=True) + EPS)
    yr = yr_ref[...]
    for c0 in range(0, D_MODEL, SLAB):
        p_r = _dot(yr, wr_ref[:, c0:c0 + SLAB])
        p_s = _dot(ys, ws_ref[:, c0:c0 + SLAB]) * rinv
        merged = (gr_ref[:, c0:c0 + SLAB].astype(f32) * p_r
                  + gs_ref[:, c0:c0 + SLAB].astype(f32) * p_s)
        o_ref[:, c0:c0 + SLAB] = merged.astype(o_ref.dtype)


def _merge(y_r, y_s, gates, w_r, w_s):
    t = y_r.shape[0]
    tm = min(TM_MERGE, t)
    assert t % tm == 0
    const = lambda i: (0, 0)
    return pl.pallas_call(
        _merge_body,
        grid=(t // tm,),
        in_specs=[
            pl.BlockSpec((tm, RET_WIDTH), lambda i: (i, 0)),
            pl.BlockSpec((tm, SSD_WIDTH), lambda i: (i, 0)),
            pl.BlockSpec((tm, D_MODEL), lambda i: (i, 0)),
            pl.BlockSpec((tm, D_MODEL), lambda i: (i, 1)),
            pl.BlockSpec((RET_WIDTH, D_MODEL), const, pipeline_mode=pl.Buffered(1)),
            pl.BlockSpec((SSD_WIDTH, D_MODEL), const, pipeline_mode=pl.Buffered(1)),
        ],
        out_specs=pl.BlockSpec((tm, D_MODEL), lambda i: (i, 0)),
        out_shape=jax.ShapeDtypeStruct((t, D_MODEL), bf16),
        compiler_params=_params(1),
        name="merge",
    )(y_r, y_s, gates, gates, w_r, w_s)


def _out_body(x_ref, m_ref, wo_ref, nw_ref, o_ref):
    xo = x_ref[...] + _dot(m_ref[...], wo_ref[...])
    ms = jnp.mean(xo * xo, axis=-1, keepdims=True)
    o_ref[...] = xo * lax.rsqrt(ms + EPS) * nw_ref[...]


def _out_proj(x2, merged, w_o, norm_f_w):
    t = x2.shape[0]
    tm = min(TM_OUT, t)
    assert t % tm == 0
    const = lambda i: (0, 0)
    return pl.pallas_call(
        _out_body,
        grid=(t // tm,),
        in_specs=[
            pl.BlockSpec((tm, D_MODEL), lambda i: (i, 0)),
            pl.BlockSpec((tm, D_MODEL), lambda i: (i, 0)),
            pl.BlockSpec((D_MODEL, D_MODEL), const, pipeline_mode=pl.Buffered(1)),
            pl.BlockSpec((1, D_MODEL), const),
        ],
        out_specs=pl.BlockSpec((tm, D_MODEL), lambda i: (i, 0)),
        out_shape=jax.ShapeDtypeStruct((t, D_MODEL), f32),
        compiler_params=_params(1),
        name="out_proj",
    )(x2, merged, w_o, norm_f_w)


def kernel(x, positions, norm1_w, w_in, conv_w, conv_b, dt_bias, a_log, d_skip,
           ssd_norm_w, w_br_ret, w_br_ssd, w_out, norm_f_w):
    batch, seq, d = x.shape
    assert d == D_MODEL and w_in.shape[0] == 1
    t = batch * seq
    x2 = x.reshape(t, d)
    w = jnp.swapaxes(w_in, 1, 2)

    w_gates = w[:, OFF_GATES:, :]
    pos_b = jnp.broadcast_to(positions.astype(f32).reshape(t, 1), (t, ROPE_HALF))
    inv_freq = (ROPE_THETA ** (-jnp.arange(ROPE_HALF, dtype=f32) / ROPE_HALF)).reshape(1, ROPE_HALF)

    h, dt_t, cos, sin = _prenorm(x2, norm1_w[0].reshape(1, d), w, pos_b, inv_freq)
    qk = _proj(h, w, OFF_Q, 2 * RET_WIDTH, "rope", seq, (cos, sin))
    v = _proj(h, w, OFF_V, RET_WIDTH, "none", seq)
    gz = _proj(h, w, OFF_G, RET_WIDTH + SSD_WIDTH, "silu", seq)
    xbc = _proj(h, w, OFF_XBC, SSD_CONV_DIM, "conv", seq,
                (conv_w[0].astype(f32), conv_b[0].astype(f32).reshape(1, SSD_CONV_DIM)))
    gates = _proj(h, w_gates, 0, 2 * D_MODEL, "sigmoid", seq)

    y_r = _retention(qk, v, gz, batch, seq)
    y_s = _ssd(xbc, gz, dt_t, dt_bias[0], a_log[0], d_skip[0], batch, seq)
    w_s = (ssd_norm_w[0].astype(f32)[:, None] * w_br_ssd[0]).astype(bf16)
    merged = _merge(y_r, y_s, gates, w_br_ret[0].astype(bf16), w_s)
    out = _out_proj(x2, merged, w_out[0].astype(bf16), norm_f_w.reshape(1, d))
    return out.reshape(batch, seq, d)
```

```python
import functools
import math

import numpy as np
import jax
import jax.numpy as jnp
from jax import lax
from jax.experimental import pallas as pl
from jax.experimental.pallas import tpu as pltpu

f32 = jnp.float32
bf16 = jnp.bfloat16

D_MODEL = 2048
CHUNK = 64
EPS = 1e-6

RET_HEADS = 8
RET_DK = 256
RET_DV = 256
RET_WIDTH = RET_HEADS * RET_DV
ROPE_THETA = 10000.0
ROPE_HALF = RET_DK // 2

SSD_WIDTH = 2 * D_MODEL
SSD_HEADDIM = 64
SSD_HEADS = SSD_WIDTH // SSD_HEADDIM
SSD_GROUPS = 8
SSD_HPG = SSD_HEADS // SSD_GROUPS
SSD_STATE = 128
SSD_CONV = 4
SSD_GW = SSD_HPG * SSD_HEADDIM
SSD_CONV_DIM = SSD_WIDTH + 2 * SSD_GROUPS * SSD_STATE

OFF_Q = 0
OFF_K = OFF_Q + RET_WIDTH
OFF_V = OFF_K + RET_WIDTH
OFF_G = OFF_V + RET_WIDTH
OFF_Z = OFF_G + RET_WIDTH
OFF_XBC = OFF_Z + SSD_WIDTH
OFF_DT = OFF_XBC + SSD_CONV_DIM
OFF_GATES = OFF_DT + SSD_HEADS
DT_PAD = 128

LANES = 128
SUBLANES = 8
VMEM_LIMIT = 56 * 1024 * 1024
LOG2E = 1.0 / math.log(2.0)

TM_PRE = 1024
TM_IN = 1024
TN_IN = 1024
MCHUNK = 128
SLAB = 256
RET_BLK = 256
RET_HB = 8
RET_NB = 2
SSD_TB = 2048
SSD_L = 128
TM_MERGE = 256
TM_OUT = 512


def _sigmoid(x):
    return 1.0 / (1.0 + jnp.exp2(x * (-LOG2E)))


def _silu(x):
    return x * _sigmoid(x)


def _dot(a, b, ca=1, cb=0):
    return lax.dot_general(a, b, (((ca,), (cb,)), ((), ())), preferred_element_type=f32)


def _params(n_axes):
    return pltpu.CompilerParams(dimension_semantics=("arbitrary",) * n_axes,
                                vmem_limit_bytes=VMEM_LIMIT)


def _prenorm_body(x_ref, nw_ref, wdt_ref, pos_ref, freq_ref, h_ref, dt_ref, cos_ref, sin_ref):
    x = x_ref[...]
    ms = jnp.mean(x * x, axis=-1, keepdims=True)
    h = (x * lax.rsqrt(ms + EPS) * nw_ref[...]).astype(bf16)
    h_ref[...] = h
    dt_ref[...] = _dot(wdt_ref[...].astype(bf16), h, 1, 1)
    ang = pos_ref[...] * freq_ref[...]
    cos_ref[...] = jnp.cos(ang)
    sin_ref[...] = jnp.sin(ang)


def _prenorm(x2, norm_w, w_dt, pos_b, inv_freq):
    t = x2.shape[0]
    tm = min(TM_PRE, t)
    assert t % tm == 0
    const = lambda i: (0, 0)
    return pl.pallas_call(
        _prenorm_body,
        grid=(t // tm,),
        in_specs=[
            pl.BlockSpec((tm, D_MODEL), lambda i: (i, 0)),
            pl.BlockSpec((1, D_MODEL), const),
            pl.BlockSpec((None, DT_PAD, D_MODEL), lambda i: (0, OFF_DT // DT_PAD, 0)),
            pl.BlockSpec((tm, ROPE_HALF), lambda i: (i, 0)),
            pl.BlockSpec((1, ROPE_HALF), const),
        ],
        out_specs=[
            pl.BlockSpec((tm, D_MODEL), lambda i: (i, 0)),
            pl.BlockSpec((DT_PAD, tm), lambda i: (0, i)),
            pl.BlockSpec((tm, ROPE_HALF), lambda i: (i, 0)),
            pl.BlockSpec((tm, ROPE_HALF), lambda i: (i, 0)),
        ],
        out_shape=[
            jax.ShapeDtypeStruct((t, D_MODEL), bf16),
            jax.ShapeDtypeStruct((DT_PAD, t), f32),
            jax.ShapeDtypeStruct((t, ROPE_HALF), f32),
            jax.ShapeDtypeStruct((t, ROPE_HALF), f32),
        ],
        compiler_params=_params(1),
        name="prenorm",
    )(x2, norm_w, w_dt, pos_b, inv_freq)


def _proj_body(*refs, epilogue, tiles_per_seq, k_first_col_block):
    if epilogue == "rope":
        h_ref, w_ref, cos_ref, sin_ref, o_ref, wb_ref = refs
    elif epilogue == "conv":
        h_ref, w_ref, cw_ref, cb_ref, o_ref, wb_ref, halo_ref = refs
    else:
        h_ref, w_ref, o_ref, wb_ref = refs
    j = pl.program_id(0)
    i = pl.program_id(1)
    tm, tn = o_ref.shape

    @pl.when(i == 0)
    def _():
        for c in range(0, D_MODEL, SLAB):
            wb_ref[c:c + SLAB, :] = w_ref[:, c:c + SLAB].T.astype(bf16)

    if epilogue == "conv":
        @pl.when(i % tiles_per_seq == 0)
        def _():
            halo_ref[...] = jnp.zeros((SUBLANES, tn), f32)

    if epilogue == "rope":
        scale = jnp.where(j >= k_first_col_block, RET_DK ** -0.5, 1.0).astype(f32)
        cos = cos_ref[...] * scale
        sin = sin_ref[...] * scale

    for n in range(tn // SLAB):
        c0 = n * SLAB
        if epilogue == "conv":
            acc = _dot(h_ref[...], wb_ref[:, c0:c0 + SLAB])
            ext = jnp.concatenate([halo_ref[:, c0:c0 + SLAB], acc], axis=0)
            w = cw_ref[:, c0:c0 + SLAB]
            x1 = pltpu.roll(ext, 1, axis=0)
            a2 = pltpu.roll(ext * w[1:2] + x1 * w[0:1], 2, axis=0)
            out = (ext * w[3:4] + x1 * w[2:3] + a2 + cb_ref[:, c0:c0 + SLAB])[SUBLANES:]
            halo_ref[:, c0:c0 + SLAB] = acc[tm - SUBLANES:tm]
            o_ref[:, c0:c0 + SLAB] = out.astype(o_ref.dtype)
            continue
        for r0 in range(0, tm, MCHUNK):
            acc = _dot(h_ref[r0:r0 + MCHUNK, :], wb_ref[:, c0:c0 + SLAB])
            if epilogue == "none":
                out = acc
            elif epilogue == "silu":
                out = _silu(acc)
            elif epilogue == "sigmoid":
                out = _sigmoid(acc)
            elif epilogue == "rope":
                t1 = acc[:, :ROPE_HALF]
                t2 = acc[:, ROPE_HALF:]
                cs = cos[r0:r0 + MCHUNK]
                sn = sin[r0:r0 + MCHUNK]
                out = jnp.concatenate([t1 * cs - t2 * sn, t2 * cs + t1 * sn], axis=-1)
            o_ref[r0:r0 + MCHUNK, c0:c0 + SLAB] = out.astype(o_ref.dtype)


def _proj(h, w, col0, ncols, epilogue, seq, extra=()):
    t = h.shape[0]
    tm = min(TM_IN, t, seq)
    tn = TN_IN
    assert t % tm == 0 and seq % tm == 0 and ncols % tn == 0
    if col0 % tn == 0:
        jb = col0 // tn
        w_spec = pl.BlockSpec((None, tn, D_MODEL), lambda j, i: (0, jb + j, 0))
    else:
        assert col0 % SUBLANES == 0
        w = w.reshape(w.shape[1], w.shape[2])
        w_spec = pl.BlockSpec((pl.Element(tn), pl.Element(D_MODEL)),
                              lambda j, i: (pl.multiple_of(col0 + j * tn, SUBLANES), 0))
    in_specs = [pl.BlockSpec((tm, D_MODEL), lambda j, i: (i, 0)), w_spec]
    scratch = [pltpu.VMEM((D_MODEL, tn), bf16)]
    if epilogue == "rope":
        in_specs += [pl.BlockSpec((tm, ROPE_HALF), lambda j, i: (i, 0))] * 2
    elif epilogue == "conv":
        cj = (col0 - OFF_XBC) // tn
        in_specs += [pl.BlockSpec((SSD_CONV, tn), lambda j, i: (0, cj + j)),
                     pl.BlockSpec((1, tn), lambda j, i: (0, cj + j))]
        scratch.append(pltpu.VMEM((SUBLANES, tn), f32))
    body = functools.partial(_proj_body, epilogue=epilogue, tiles_per_seq=seq // tm,
                             k_first_col_block=(OFF_K - col0) // tn)
    return pl.pallas_call(
        body,
        grid=(ncols // tn, t // tm),
        in_specs=in_specs,
        out_specs=pl.BlockSpec((tm, tn), lambda j, i: (i, j)),
        out_shape=jax.ShapeDtypeStruct((t, ncols), bf16),
        scratch_shapes=scratch,
        compiler_params=_params(2),
        name="proj_" + epilogue,
    )(h, w, *extra)


def _retention_body(q_ref, k_ref, v_ref, g_ref, dmask_ref, qdec_ref, kdec_ref, cdec_ref,
                    o_ref, state_ref):
    blk = pl.program_id(1)
    h0 = pl.program_id(2) * RET_HB

    @pl.when(blk == 0)
    def _():
        for hh in range(RET_HB):
            state_ref[h0 + hh] = jnp.zeros((RET_DK, RET_DV), f32)

    for nb in range(RET_NB):
        rs = slice(nb * RET_BLK, (nb + 1) * RET_BLK)
        for hh in range(RET_HB):
            h = h0 + hh
            cs = slice(hh * RET_DK, (hh + 1) * RET_DK)
            qb = q_ref[rs, cs]
            kb = k_ref[rs, cs]
            v = v_ref[rs, cs]
            kd = (kb.astype(f32) * kdec_ref[h]).astype(bf16)

            scores = _dot(qb, kb, 1, 1) * dmask_ref[h]
            state = state_ref[h]
            y = _dot(scores.astype(bf16), v)
            y = y + _dot(qb, state.astype(bf16)) * qdec_ref[h]
            state_ref[h] = state * cdec_ref[h] + _dot(kd, v, 0, 0)

            mu = jnp.mean(y, axis=-1, keepdims=True)
            yc = y - mu
            var = jnp.mean(yc * yc, axis=-1, keepdims=True)
            yn = yc * lax.rsqrt(var + EPS)
            o_ref[rs, cs] = (yn * g_ref[rs, cs].astype(f32)).astype(o_ref.dtype)


def _retention_tables():
    log_gamma = jnp.log1p(-(2.0 ** (-5.0 - jnp.arange(RET_HEADS, dtype=f32))))
    lg = log_gamma[:, None, None]
    n = jnp.arange(RET_BLK, dtype=f32)
    diff = n[:, None] - n[None, :]
    ci = (jnp.arange(RET_BLK) // CHUNK)
    same = ci[:, None] == ci[None, :]
    later = ci[:, None] > ci[None, :]
    expo = jnp.where(same[None], jnp.abs(diff)[None], diff[None]) * lg
    dmask = jnp.where((same | later)[None], jnp.exp(expo), 0.0).astype(f32)
    qdec = jnp.exp((n[None, :, None] + 1.0) * lg)
    kdec = jnp.exp((RET_BLK - 1.0 - n[None, :, None]) * lg)
    cdec = jnp.exp(RET_BLK * lg)
    qdec = jnp.broadcast_to(qdec, (RET_HEADS, RET_BLK, RET_DV))
    kdec = jnp.broadcast_to(kdec, (RET_HEADS, RET_BLK, RET_DK))
    cdec = jnp.broadcast_to(cdec, (RET_HEADS, RET_DK, RET_DV))
    return dmask, qdec, kdec, cdec


def _retention(qk, v, gz, batch, seq):
    t = batch * seq
    rows_per_step = RET_NB * RET_BLK
    nblk = seq // rows_per_step
    assert seq % rows_per_step == 0
    dmask, qdec, kdec, cdec = _retention_tables()

    def rows(b, n, h):
        return b * nblk + n

    const3 = lambda b, n, h: (0, 0, 0)
    hgroups = RET_HEADS // RET_HB
    wide = RET_HB * RET_DK
    return pl.pallas_call(
        _retention_body,
        grid=(batch, nblk, hgroups),
        in_specs=[
            pl.BlockSpec((rows_per_step, wide), lambda b, n, h: (rows(b, n, h), h)),
            pl.BlockSpec((rows_per_step, wide), lambda b, n, h: (rows(b, n, h), hgroups + h)),
            pl.BlockSpec((rows_per_step, wide), lambda b, n, h: (rows(b, n, h), h)),
            pl.BlockSpec((rows_per_step, wide), lambda b, n, h: (rows(b, n, h), h)),
            pl.BlockSpec((RET_HEADS, RET_BLK, RET_BLK), const3),
            pl.BlockSpec((RET_HEADS, RET_BLK, RET_DV), const3),
            pl.BlockSpec((RET_HEADS, RET_BLK, RET_DK), const3),
            pl.BlockSpec((RET_HEADS, RET_DK, RET_DV), const3),
        ],
        out_specs=pl.BlockSpec((rows_per_step, wide), lambda b, n, h: (rows(b, n, h), h)),
        out_shape=jax.ShapeDtypeStruct((t, RET_WIDTH), bf16),
        scratch_shapes=[pltpu.VMEM((RET_HEADS, RET_DK, RET_DV), f32)],
        compiler_params=_params(3),
        name="retention",
    )(qk, qk, v, gz, dmask, qdec, kdec, cdec)


def _split3(a):
    hi = a.astype(bf16).astype(f32)
    r = a - hi
    mid = r.astype(bf16).astype(f32)
    lo = (r - mid).astype(bf16).astype(f32)
    return [hi, mid, lo]


SSD_KGRP = 32
SSD_K_ACUM, SSD_K_ONES, SSD_K_EEXP, SSD_K_TAIL = 0, 32, 64, 96


def _group32(parts):
    return jnp.concatenate(list(parts) + [jnp.zeros_like(parts[0])], axis=0)


def _ssd_decay_prep(dt_raw, bias, neg_a, upper, blockdiag):
    H, L = dt_raw.shape
    raw = dt_raw + bias
    dt = jnp.maximum(raw, 0.0) + jnp.log1p(jnp.exp(-jnp.abs(raw)))
    csum = _dot(jnp.concatenate(_split3(dt * neg_a), axis=0).astype(bf16), upper)
    acum = (csum[0:H] + csum[H:2 * H]) + csum[2 * H:3 * H]
    alast = acum[:, L - 1:L]
    tailw = jnp.exp(alast - acum) * dt
    acum2 = acum * LOG2E
    srow = acum2 - jnp.maximum(jnp.log(dt) * LOG2E, -200.0)
    ones = jnp.ones_like(acum2)
    stack = jnp.concatenate([_group32(_split3(acum2)), _group32([ones, ones, ones]),
                             _group32(_split3(jnp.exp2(acum2))), _group32(_split3(tailw))], axis=0)
    lhs = stack.T.astype(bf16)
    dyn = jnp.tile(_group32([-p for p in _split3(srow)]), (1, H))
    dyn = jnp.where(blockdiag, dyn, 0.0).astype(bf16)
    return lhs, dyn


def _ssd_body(xs_ref, b_ref, c_ref, z_ref, dt_ref, dtn_ref, bias_ref, alog_ref, dexp_ref,
              rconst_ref, rexp_ref, o_ref, state_ref, lhs_ref, rhs_ref):
    blk = pl.program_id(2)
    tb = xs_ref.shape[0]
    L = SSD_L
    nsb = tb // L
    H = SSD_HPG
    GW = SSD_GW

    neg_a = -jnp.exp(alog_ref[...])
    bias = bias_ref[...]
    row = lax.broadcasted_iota(jnp.int32, (L, L), 0)
    col = lax.broadcasted_iota(jnp.int32, (L, L), 1)
    causal = row >= col
    upper = jnp.where(row <= col, 1.0, 0.0).astype(bf16)
    lane_lo = lax.broadcasted_iota(jnp.int32, (L, LANES), 1) < SSD_HEADDIM
    lane_lo2 = jnp.concatenate([lane_lo, lane_lo], axis=0)
    row_top = lax.broadcasted_iota(jnp.int32, (2 * L, LANES), 0) < L
    keep_bd = row_top == lane_lo2
    krow = lax.broadcasted_iota(jnp.int32, (SSD_KGRP, H * L), 0)
    kcol = lax.broadcasted_iota(jnp.int32, (SSD_KGRP, H * L), 1)
    blockdiag = (krow % H) == (kcol // L)

    def prep(dt_raw):
        return _ssd_decay_prep(dt_raw, bias, neg_a, upper, blockdiag)

    @pl.when(blk == 0)
    def _():
        state_ref[...] = jnp.zeros(state_ref.shape, f32)
        for sb in range(nsb):
            lhs0, dyn0 = prep(dt_ref[:, sb * L:(sb + 1) * L])
            lhs_ref[sb] = lhs0
            rhs_ref[sb] = rconst_ref[...]
            rhs_ref[sb, SSD_K_ONES:SSD_K_ONES + SSD_KGRP, :] = dyn0

    for sb in range(nsb):
        t0 = sb * L
        lhs = lhs_ref[sb]
        segall = _dot(lhs, rhs_ref[sb])
        expall = _dot(lhs, rexp_ref[...])
        lhs_next, dyn_next = prep(dtn_ref[:, t0:t0 + L])
        xc = _silu(xs_ref[t0:t0 + L, :].astype(f32))
        xcb = xc.astype(bf16)
        bcb = _silu(b_ref[t0:t0 + L, :].astype(f32)).astype(bf16)
        ccb = _silu(c_ref[t0:t0 + L, :].astype(f32)).astype(bf16)

        cb = _dot(ccb, bcb, 1, 1)

        ypieces = []
        for j in range(SSD_HPG // 2):
            gs = []
            for hh in (2 * j, 2 * j + 1):
                seg = segall[:, hh * L:(hh + 1) * L]
                gs.append((cb * jnp.exp2(jnp.where(causal, seg, -jnp.inf))).astype(bf16))
            gpair = jnp.concatenate(gs, axis=1)
            xpair = xcb[:, j * LANES:(j + 1) * LANES]
            x2 = jnp.concatenate([xpair, xpair], axis=0)
            xbd = jnp.where(keep_bd, x2, jnp.zeros_like(x2))
            ypieces.append(_dot(gpair, xbd))
        y = jnp.concatenate(ypieces, axis=-1)

        state = state_ref[...]
        ycross = _dot(ccb, state.astype(bf16))
        y = y + ycross * expall[:, :GW]
        y = y + dexp_ref[...] * xc

        xw = (xc * expall[:, GW:]).astype(bf16)
        sdec = expall[L - 1:L, :GW]
        state_ref[...] = state * sdec + _dot(bcb, xw, 0, 0)

        o_ref[t0:t0 + L, :] = (y * z_ref[t0:t0 + L, :].astype(f32)).astype(o_ref.dtype)
        lhs_ref[sb] = lhs_next
        rhs_ref[sb, SSD_K_ONES:SSD_K_ONES + SSD_KGRP, :] = dyn_next


def _ssd(xbc, gz, dt_t, dt_bias, a_log, d_skip, batch, seq):
    t = batch * seq
    tb = min(SSD_TB, seq)
    nblk = seq // tb
    assert seq % tb == 0 and tb % SSD_L == 0
    L = SSD_L
    bias_b = jnp.broadcast_to(dt_bias.astype(f32)[:, None], (SSD_HEADS, L))
    alog_b = jnp.broadcast_to(a_log.astype(f32)[:, None], (SSD_HEADS, L))
    d_exp = jnp.repeat(d_skip.astype(f32), SSD_HEADDIM).reshape(1, SSD_WIDTH)

    kk = np.arange(4 * SSD_KGRP)
    k_head = kk % SSD_HPG
    k_live = (kk % SSD_KGRP) < 3 * SSD_HPG
    k_grp = kk // SSD_KGRP
    seg_lane_head = np.arange(SSD_HPG * L) // L
    rconst = ((k_grp == SSD_K_ACUM // SSD_KGRP) & k_live)[:, None] & (k_head[:, None] == seg_lane_head[None, :])
    x_lane_head = np.arange(SSD_GW) // SSD_HEADDIM
    hit = k_live[:, None] & (k_head[:, None] == x_lane_head[None, :])
    rexp = np.concatenate([hit & (k_grp == SSD_K_EEXP // SSD_KGRP)[:, None],
                           hit & (k_grp == SSD_K_TAIL // SSD_KGRP)[:, None]], axis=1)
    rconst = jnp.asarray(rconst, dtype=bf16)
    rexp = jnp.asarray(rexp, dtype=bf16)

    cz = RET_WIDTH // SSD_GW
    cbm = SSD_WIDTH // SSD_STATE
    ccm = cbm + SSD_GROUPS

    def rows(b, g, n):
        return b * nblk + n

    def next_rows(b, g, n):
        return jnp.minimum(b * nblk + n + 1, batch * nblk - 1)

    return pl.pallas_call(
        _ssd_body,
        grid=(batch, SSD_GROUPS, nblk),
        in_specs=[
            pl.BlockSpec((tb, SSD_GW), lambda b, g, n: (rows(b, g, n), g)),
            pl.BlockSpec((tb, SSD_STATE), lambda b, g, n: (rows(b, g, n), cbm + g)),
            pl.BlockSpec((tb, SSD_STATE), lambda b, g, n: (rows(b, g, n), ccm + g)),
            pl.BlockSpec((tb, SSD_GW), lambda b, g, n: (rows(b, g, n), cz + g)),
            pl.BlockSpec((SSD_HPG, tb), lambda b, g, n: (g, rows(b, g, n))),
            pl.BlockSpec((SSD_HPG, tb), lambda b, g, n: (g, next_rows(b, g, n))),
            pl.BlockSpec((SSD_HPG, L), lambda b, g, n: (g, 0)),
            pl.BlockSpec((SSD_HPG, L), lambda b, g, n: (g, 0)),
            pl.BlockSpec((1, SSD_GW), lambda b, g, n: (0, g)),
            pl.BlockSpec((4 * SSD_KGRP, SSD_HPG * L), lambda b, g, n: (0, 0)),
            pl.BlockSpec((4 * SSD_KGRP, 2 * SSD_GW), lambda b, g, n: (0, 0)),
        ],
        out_specs=pl.BlockSpec((tb, SSD_GW), lambda b, g, n: (rows(b, g, n), g)),
        out_shape=jax.ShapeDtypeStruct((t, SSD_WIDTH), bf16),
        scratch_shapes=[pltpu.VMEM((SSD_STATE, SSD_GW), f32),
                        pltpu.VMEM((tb // L, L, 4 * SSD_KGRP), bf16),
                        pltpu.VMEM((tb // L, 4 * SSD_KGRP, SSD_HPG * L), bf16)],
        compiler_params=_params(3),
        name="ssd",
    )(xbc, xbc, xbc, gz, dt_t, dt_t, bias_b, alog_b, d_exp, rconst, rexp)


def _merge_body(yr_ref, ys_ref, gr_ref, gs_ref, wr_ref, ws_ref, o_ref):
    ys = ys_ref[...]
    ysf = ys.astype(f32)
    rinv = lax.rsqrt(jnp.mean(ysf * ysf, axis=-1, keepdims=True) + EPS)
    yr = yr_ref[...]
    for c0 in range(0, D_MODEL, SLAB):
        p_r = _dot(yr, wr_ref[:, c0:c0 + SLAB])
        p_s = _dot(ys, ws_ref[:, c0:c0 + SLAB]) * rinv
        merged = (gr_ref[:, c0:c0 + SLAB].astype(f32) * p_r
                  + gs_ref[:, c0:c0 + SLAB].astype(f32) * p_s)
        o_ref[:, c0:c0 + SLAB] = merged.astype(o_ref.dtype)


def _merge(y_r, y_s, gates, w_r, w_s):
    t = y_r.shape[0]
    tm = min(TM_MERGE, t)
    assert t % tm == 0
    const = lambda i: (0, 0)
    return pl.pallas_call(
        _merge_body,
        grid=(t // tm,),
        in_specs=[
            pl.BlockSpec((tm, RET_WIDTH), lambda i: (i, 0)),
            pl.BlockSpec((tm, SSD_WIDTH), lambda i: (i, 0)),
            pl.BlockSpec((tm, D_MODEL), lambda i: (i, 0)),
            pl.BlockSpec((tm, D_MODEL), lambda i: (i, 1)),
            pl.BlockSpec((RET_WIDTH, D_MODEL), const, pipeline_mode=pl.Buffered(1)),
            pl.BlockSpec((SSD_WIDTH, D_MODEL), const, pipeline_mode=pl.Buffered(1)),
        ],
        out_specs=pl.BlockSpec((tm, D_MODEL), lambda i: (i, 0)),
        out_shape=jax.ShapeDtypeStruct((t, D_MODEL), bf16),
        compiler_params=_params(1),
        name="merge",
    )(y_r, y_s, gates, gates, w_r, w_s)


def _out_body(x_ref, m_ref, wo_ref, nw_ref, o_ref):
    xo = x_ref[...] + _dot(m_ref[...], wo_ref[...])
    ms = jnp.mean(xo * xo, axis=-1, keepdims=True)
    o_ref[...] = xo * lax.rsqrt(ms + EPS) * nw_ref[...]


def _out_proj(x2, merged, w_o, norm_f_w):
    t = x2.shape[0]
    tm = min(TM_OUT, t)
    assert t % tm == 0
    const = lambda i: (0, 0)
    return pl.pallas_call(
        _out_body,
        grid=(t // tm,),
        in_specs=[
            pl.BlockSpec((tm, D_MODEL), lambda i: (i, 0)),
            pl.BlockSpec((tm, D_MODEL), lambda i: (i, 0)),
            pl.BlockSpec((D_MODEL, D_MODEL), const, pipeline_mode=pl.Buffered(1)),
            pl.BlockSpec((1, D_MODEL), const),
        ],
        out_specs=pl.BlockSpec((tm, D_MODEL), lambda i: (i, 0)),
        out_shape=jax.ShapeDtypeStruct((t, D_MODEL), f32),
        compiler_params=_params(1),
        name="out_proj",
    )(x2, merged, w_o, norm_f_w)


def kernel(x, positions, norm1_w, w_in, conv_w, conv_b, dt_bias, a_log, d_skip,
           ssd_norm_w, w_br_ret, w_br_ssd, w_out, norm_f_w):
    batch, seq, d = x.shape
    assert d == D_MODEL and w_in.shape[0] == 1
    t = batch * seq
    x2 = x.reshape(t, d)
    w = jnp.swapaxes(w_in, 1, 2)

    pos_b = jnp.broadcast_to(positions.astype(f32).reshape(t, 1), (t, ROPE_HALF))
    inv_freq = (ROPE_THETA ** (-jnp.arange(ROPE_HALF, dtype=f32) / ROPE_HALF)).reshape(1, ROPE_HALF)

    h, dt_t, cos, sin = _prenorm(x2, norm1_w[0].reshape(1, d), w, pos_b, inv_freq)
    qk = _proj(h, w, OFF_Q, 2 * RET_WIDTH, "rope", seq, (cos, sin))
    v = _proj(h, w, OFF_V, RET_WIDTH, "none", seq)
    gz = _proj(h, w, OFF_G, RET_WIDTH + SSD_WIDTH, "silu", seq)
    xbc = _proj(h, w, OFF_XBC, SSD_CONV_DIM, "conv", seq,
                (conv_w[0].astype(f32), conv_b[0].astype(f32).reshape(1, SSD_CONV_DIM)))
    gates = _proj(h, w, OFF_GATES, 2 * D_MODEL, "sigmoid", seq)

    y_r = _retention(qk, v, gz, batch, seq)
    y_s = _ssd(xbc, gz, dt_t, dt_bias[0], a_log[0], d_skip[0], batch, seq)
    w_s = (ssd_norm_w[0].astype(f32)[:, None] * w_br_ssd[0]).astype(bf16)
    merged = _merge(y_r, y_s, gates, w_br_ret[0].astype(bf16), w_s)
    out = _out_proj(x2, merged, w_out[0].astype(bf16), norm_f_w.reshape(1, d))
    return out.reshape(batch, seq, d)
```
